```python
import jax, jax.numpy as jnp
from jax import lax
import numpy as np

D_MODEL = 4096
BATCH = 1
SEQ = 16384
DEPTH = 4

GRID_W = 64
CTX_LEN = 256

CONV_W = D_MODEL // 4
CONV_K = 31
POOL_W = D_MODEL // 4
POOL_WINDOWS = (2, 4, 8, 16)
POOL_GROUPS = len(POOL_WINDOWS)
POOL_GW = POOL_W // POOL_GROUPS
NA_HEAD_DIM = 128
NA_W = D_MODEL // 2
NA_HEADS = NA_W // NA_HEAD_DIM
NA_KH = 8
NA_KW = 16
N_BRANCH = 3

OFF_CONV_A = 0
OFF_CONV_G = OFF_CONV_A + CONV_W
OFF_POOL = OFF_CONV_G + CONV_W
OFF_Q = OFF_POOL + POOL_W
OFF_K = OFF_Q + NA_W
OFF_V = OFF_K + NA_W
OFF_GATE = OFF_V + NA_W
IN_W = OFF_GATE + N_BRANCH * D_MODEL

ADA_RANK = D_MODEL // 16
N_MOD = 6

N_GROUPS = 4
EXP_PER_GROUP = 8
N_EXPERTS = N_GROUPS * EXP_PER_GROUP
TOP_K_IN_GROUP = 2
EXP_HIDDEN = D_MODEL * 3 // 64

ROPE_THETA = 10000.0
ROPE_AXIS_DIM = NA_HEAD_DIM // 2
LN_EPS = 1e-5
DEEPNORM_ALPHA = (2 * DEPTH) ** 0.25
DEEPNORM_BETA = (8 * DEPTH) ** -0.25

kernel_name = 'hybrid_conv_pool_natten_hmoe_dit'


def layer_norm(x, g, b):
    xf = x.astype(jnp.float32)
    mu = jnp.mean(xf, axis=-1, keepdims=True)
    var = jnp.mean(jnp.square(xf - mu), axis=-1, keepdims=True)
    return ((xf - mu) * lax.rsqrt(var + LN_EPS)).astype(x.dtype) * g + b


def adaln(cvec, w_down, w_up, b):
    m = (jax.nn.silu(cvec) @ w_down) @ w_up + b
    return jnp.split(m, N_MOD, axis=-1)


def modulate(x, shift, scale):
    return x * (1 + scale[:, None, :]) + shift[:, None, :]


def axial_rope_tables(rows):
    t = jnp.arange(rows * GRID_W)
    r = (t // GRID_W).astype(jnp.float32)
    col = (t % GRID_W).astype(jnp.float32)
    inv = ROPE_THETA ** (-jnp.arange(0, ROPE_AXIS_DIM, 2, dtype=jnp.float32) / ROPE_AXIS_DIM)
    ang_r = r[:, None] * inv[None, :]
    ang_c = col[:, None] * inv[None, :]
    return (jnp.cos(ang_r), jnp.sin(ang_r), jnp.cos(ang_c), jnp.sin(ang_c))


def _rotate(xa, cos, sin):
    x1, x2 = jnp.split(xa, 2, axis=-1)
    cos = cos[:, None, :].astype(xa.dtype)
    sin = sin[:, None, :].astype(xa.dtype)
    return jnp.concatenate([x1 * cos - x2 * sin, x1 * sin + x2 * cos], axis=-1)


def axial_rope(x, rope):
    cos_r, sin_r, cos_c, sin_c = rope
    xr, xc = jnp.split(x, 2, axis=-1)
    return jnp.concatenate([_rotate(xr, cos_r, sin_r), _rotate(xc, cos_c, sin_c)], axis=-1)


def split_heads(p, off):
    return p[..., off:off + NA_W].reshape(p.shape[0], p.shape[1], NA_HEADS, NA_HEAD_DIM)


def depthwise_conv(x, w, b):
    y = lax.conv_general_dilated(x, w[:, None, :], window_strides=(1,),
                                 padding=[(CONV_K // 2, CONV_K // 2)],
                                 dimension_numbers=('NWC', 'WIO', 'NWC'),
                                 feature_group_count=x.shape[-1])
    return y + b


def multiscale_pool_minus_self(z):
    B, L, _ = z.shape
    zf = z.astype(jnp.float32)
    cs = jnp.concatenate([jnp.zeros((B, 1, POOL_W), jnp.float32), jnp.cumsum(zf, axis=1)], axis=1)
    pos = jnp.arange(L)
    outs = []
    for g, w in enumerate(POOL_WINDOWS):
        lo = jnp.clip(pos - w // 2, 0, L)
        hi = jnp.clip(pos + (w - w // 2), 0, L)
        sl = slice(g * POOL_GW, (g + 1) * POOL_GW)
        win_sum = cs[:, hi, sl] - cs[:, lo, sl]
        mean = win_sum / (hi - lo).astype(jnp.float32)[None, :, None]
        outs.append(mean - zf[:, :, sl])
    return jnp.concatenate(outs, axis=-1).astype(z.dtype)


def neighbourhood_attention(q_l, k_l, v_l, k_c, v_c, rpb, rows, rope):
    B = q_l.shape[0]
    kh = min(NA_KH, rows)
    scale = NA_HEAD_DIM ** -0.5
    grid = (B, rows, GRID_W, NA_HEADS, NA_HEAD_DIM)
    q_grid = axial_rope(q_l, rope).reshape(grid)
    k_grid = axial_rope(k_l, rope).reshape(grid)
    qp_grid = q_l.reshape(grid)
    v_grid = v_l.reshape(grid)
    cols = jnp.arange(GRID_W)
    col_start = jnp.clip(cols - NA_KW // 2, 0, GRID_W - NA_KW)
    col_idx = col_start[:, None] + jnp.arange(NA_KW)[None, :]
    dc = col_idx - cols[:, None] + (NA_KW - 1)

    def row_block(r):
        r0 = jnp.clip(r - kh // 2, 0, rows - kh)
        k_rows = lax.dynamic_slice_in_dim(k_grid, r0, kh, axis=1)
        v_rows = lax.dynamic_slice_in_dim(v_grid, r0, kh, axis=1)
        k_win = k_rows[:, :, col_idx]
        v_win = v_rows[:, :, col_idx]
        q_row = lax.dynamic_index_in_dim(q_grid, r, axis=1, keepdims=False)
        qp_row = lax.dynamic_index_in_dim(qp_grid, r, axis=1, keepdims=False)
        dr = r0 + jnp.arange(kh) - r + (NA_KH - 1)
        bias = rpb[:, dr[None, :, None], dc[:, None, :]]
        s_lat = jnp.einsum('bwhd,biwjhd->bhwij', q_row, k_win).astype(jnp.float32) * scale + bias
        s_ctx = jnp.einsum('bwhd,bmhd->bhwm', qp_row, k_c).astype(jnp.float32) * scale
        s = jnp.concatenate([s_lat.reshape(B, NA_HEADS, GRID_W, kh * NA_KW), s_ctx], axis=-1)
        p = jax.nn.softmax(s, axis=-1).astype(v_l.dtype)
        p_lat = p[..., :kh * NA_KW].reshape(B, NA_HEADS, GRID_W, kh, NA_KW)
        p_ctx = p[..., kh * NA_KW:]
        return (jnp.einsum('bhwij,biwjhd->bwhd', p_lat, v_win)
                + jnp.einsum('bhwm,bmhd->bwhd', p_ctx, v_c))

    out = lax.map(row_block, jnp.arange(rows))
    return jnp.moveaxis(out, 0, 1).reshape(B, rows * GRID_W, NA_W)


def context_attention(q_c, k_c, v_c):
    B, M = q_c.shape[0], q_c.shape[1]
    s = jnp.einsum('bqhd,bkhd->bhqk', q_c, k_c).astype(jnp.float32) * (NA_HEAD_DIM ** -0.5)
    p = jax.nn.softmax(s, axis=-1).astype(v_c.dtype)
    return jnp.einsum('bhqk,bkhd->bqhd', p, v_c).reshape(B, M, NA_W)


def merge_branches(p, na_out, dw, dw_b, ln_g, ln_b, w_conv_out, pool_w, pool_scale, w_na_out, w_out):
    B, L, _ = p.shape
    glu = p[..., OFF_CONV_A:OFF_CONV_G] * jax.nn.sigmoid(p[..., OFF_CONV_G:OFF_POOL])
    h = jax.nn.silu(layer_norm(depthwise_conv(glu, dw, dw_b), ln_g, ln_b))
    y_conv = h @ w_conv_out
    d = multiscale_pool_minus_self(p[..., OFF_POOL:OFF_Q])
    y_pool = jnp.einsum('blgc,gcd->blgd', d.reshape(B, L, POOL_GROUPS, POOL_GW),
                        pool_w).reshape(B, L, D_MODEL) * pool_scale
    y_na = na_out @ w_na_out
    gates = jax.nn.sigmoid(p[..., OFF_GATE:IN_W].reshape(B, L, N_BRANCH, D_MODEL))
    merged = gates[..., 0, :] * y_conv + gates[..., 1, :] * y_pool + gates[..., 2, :] * y_na
    return merged @ w_out


def hier_moe(u, rg_w, rg_b, re_w, re_b, w_gate, w_up, w_down):
    B, L, D = u.shape
    t = u.reshape(B * L, D)
    T = t.shape[0]
    g_logits = (t @ rg_w + rg_b).astype(jnp.float32)
    g_prob = jax.nn.softmax(g_logits, axis=-1)
    _, g_idx = lax.top_k(g_logits, 1)
    p_group = jnp.take_along_axis(g_prob, g_idx, axis=-1)
    e_logits = (t @ re_w + re_b).astype(jnp.float32).reshape(T, N_GROUPS, EXP_PER_GROUP)
    sel = jnp.broadcast_to(g_idx[:, :, None], (T, 1, EXP_PER_GROUP))
    e_in_group = jnp.take_along_axis(e_logits, sel, axis=1)[:, 0]
    e_top, e_idx = lax.top_k(e_in_group, TOP_K_IN_GROUP)
    w_pair = jax.nn.softmax(e_top, axis=-1) * p_group
    expert_id = g_idx * EXP_PER_GROUP + e_idx
    combine = jnp.einsum('tk,tke->te', w_pair,
                         jax.nn.one_hot(expert_id, N_EXPERTS, dtype=jnp.float32))
    h = jax.nn.silu(jnp.einsum('td,edh->teh', t, w_gate)) * jnp.einsum('td,edh->teh', t, w_up)
    h = h * combine[:, :, None].astype(h.dtype)
    return jnp.einsum('teh,ehd->td', h, w_down).reshape(B, L, D)


def setup_inputs(seed: int = 0) -> dict:
    key = jax.random.key(seed)
    keys = jax.random.split(key, 40)
    counter = [0]

    def nrm(shape, scale):
        k = keys[counter[0]]
        counter[0] += 1
        return jax.random.normal(k, shape, jnp.float32) * scale

    L, D = DEPTH, D_MODEL
    inp = {}
    inp['x'] = nrm((BATCH, SEQ, D), 1.0)
    inp['c'] = nrm((BATCH, D), 1.0)
    inp['ctx'] = nrm((BATCH, CTX_LEN, D), 1.0)
    inp['c_ctx'] = nrm((D,), 1.0)
    inp['ada_w_down'] = nrm((L, D, ADA_RANK), D ** -0.5)
    inp['ada_w_up'] = nrm((L, ADA_RANK, N_MOD * D), 0.5 * ADA_RANK ** -0.5)
    inp['ada_b'] = nrm((L, N_MOD * D), 0.02)
    inp['w_in'] = nrm((L, D, IN_W), D ** -0.5)
    inp['conv_dw'] = nrm((L, CONV_K, CONV_W), CONV_K ** -0.5)
    inp['conv_dw_b'] = nrm((L, CONV_W), 0.02)
    inp['conv_ln_g'] = 1.0 + nrm((L, CONV_W), 0.02)
    inp['conv_ln_b'] = nrm((L, CONV_W), 0.02)
    inp['w_conv_out'] = nrm((L, CONV_W, D), CONV_W ** -0.5)
    inp['pool_w'] = nrm((L, POOL_GROUPS, POOL_GW, D // POOL_GROUPS), POOL_GW ** -0.5)
    inp['pool_scale'] = 1.0 + nrm((L, D), 0.02)
    inp['na_rpb'] = nrm((L, NA_HEADS, 2 * NA_KH - 1, 2 * NA_KW - 1), 0.05)
    inp['w_na_out'] = nrm((L, NA_W, D), NA_W ** -0.5)
    inp['w_out'] = nrm((L, D, D), DEEPNORM_BETA * D ** -0.5)
    inp['ln1_g'] = 1.0 + nrm((L, D), 0.02)
    inp['ln1_b'] = nrm((L, D), 0.02)
    inp['router_g_w'] = nrm((L, D, N_GROUPS), D ** -0.5)
    inp['router_g_b'] = nrm((L, N_GROUPS), 0.01)
    inp['router_e_w'] = nrm((L, D, N_EXPERTS), D ** -0.5)
    inp['router_e_b'] = nrm((L, N_EXPERTS), 0.01)
    inp['moe_w_gate'] = nrm((L, N_EXPERTS, D, EXP_HIDDEN), D ** -0.5)
    inp['moe_w_up'] = nrm((L, N_EXPERTS, D, EXP_HIDDEN), D ** -0.5)
    inp['moe_w_down'] = nrm((L, N_EXPERTS, EXP_HIDDEN, D), DEEPNORM_BETA * EXP_HIDDEN ** -0.5)
    inp['ln2_g'] = 1.0 + nrm((L, D), 0.02)
    inp['ln2_b'] = nrm((L, D), 0.02)
    return inp


def reference(x, c, ctx, c_ctx, ada_w_down, ada_w_up, ada_b, w_in, conv_dw, conv_dw_b, conv_ln_g,
              conv_ln_b, w_conv_out, pool_w, pool_scale, na_rpb, w_na_out, w_out, ln1_g, ln1_b,
              router_g_w, router_g_b, router_e_w, router_e_b, moe_w_gate, moe_w_up, moe_w_down,
              ln2_g, ln2_b):
    rows = x.shape[1] // GRID_W
    rope = axial_rope_tables(rows)
    xl, xc = x, ctx
    for l in range(DEPTH):
        last = l == DEPTH - 1
        sh1, sc1, g1, sh2, sc2, g2 = adaln(c, ada_w_down[l], ada_w_up[l], ada_b[l])
        sh1c, sc1c, g1c, sh2c, sc2c, g2c = adaln(c_ctx[None, :], ada_w_down[l], ada_w_up[l], ada_b[l])
        branch_w = (conv_dw[l], conv_dw_b[l], conv_ln_g[l], conv_ln_b[l], w_conv_out[l],
                    pool_w[l], pool_scale[l], w_na_out[l], w_out[l])
        moe_w = (router_g_w[l], router_g_b[l], router_e_w[l], router_e_b[l],
                 moe_w_gate[l], moe_w_up[l], moe_w_down[l])

        pl = modulate(xl, sh1, sc1) @ w_in[l]
        pc = modulate(xc, sh1c, sc1c) @ w_in[l]
        k_c, v_c = split_heads(pc, OFF_K), split_heads(pc, OFF_V)
        na_l = neighbourhood_attention(split_heads(pl, OFF_Q), split_heads(pl, OFF_K),
                                       split_heads(pl, OFF_V), k_c, v_c, na_rpb[l], rows, rope)
        y_l = merge_branches(pl, na_l, *branch_w)
        xl = layer_norm(DEEPNORM_ALPHA * xl + g1[:, None, :] * y_l, ln1_g[l], ln1_b[l])
        if not last:
            na_c = context_attention(split_heads(pc, OFF_Q), k_c, v_c)
            y_c = merge_branches(pc, na_c, *branch_w)
            xc = layer_norm(DEEPNORM_ALPHA * xc + g1c[:, None, :] * y_c, ln1_g[l], ln1_b[l])

        f_l = hier_moe(modulate(xl, sh2, sc2), *moe_w)
        xl = layer_norm(DEEPNORM_ALPHA * xl + g2[:, None, :] * f_l, ln2_g[l], ln2_b[l])
        if not last:
            f_c = hier_moe(modulate(xc, sh2c, sc2c), *moe_w)
            xc = layer_norm(DEEPNORM_ALPHA * xc + g2c[:, None, :] * f_c, ln2_g[l], ln2_b[l])
    return xl
```

```python
import functools

import jax
import jax.numpy as jnp
from jax import lax
from jax.experimental import pallas as pl
from jax.experimental.pallas import tpu as pltpu

F32 = jnp.float32
BF16 = jnp.bfloat16
HIGHEST = lax.Precision.HIGHEST

GRID_W = 64
CONV_K = 31
POOL_WINDOWS = (2, 4, 8, 16)
HEAD_DIM = 128
NA_KH = 8
NA_KW = 16
N_GROUPS = 4
EXP_PER_GROUP = 8
N_EXPERTS = N_GROUPS * EXP_PER_GROUP
N_MOD = 6
ROPE_THETA = 10000.0
LN_EPS = 1e-5
NEG = -1e30

VMEM_LIMIT_BYTES = 56 * 1024 * 1024
LANES = 128
HALO = 16
ROUTER_LANES = 128
EXPERTS_PER_CHUNK = 2


def _params(sem):
    return pltpu.CompilerParams(dimension_semantics=sem, vmem_limit_bytes=VMEM_LIMIT_BYTES)


def _pick(n, cands):
    for c in cands:
        if n % c == 0:
            return c
    raise ValueError(f"no tile in {cands} divides {n}")


def _sigmoid(x):
    return 1.0 / (1.0 + jnp.exp(-x))


def _adaln_kernel(c_ref, wd_ref, wu_ref, b_ref, o_ref):
    c = c_ref[...]
    s = c * _sigmoid(c)
    t = jnp.dot(s, wd_ref[...], precision=HIGHEST, preferred_element_type=F32)
    o_ref[...] = jnp.dot(t, wu_ref[...], precision=HIGHEST, preferred_element_type=F32) + b_ref[...]


def _adaln(cc, w_down, w_up, b):
    L, D, R = w_down.shape
    return pl.pallas_call(
        _adaln_kernel,
        out_shape=jax.ShapeDtypeStruct((L, 8, N_MOD * D), F32),
        grid=(L, N_MOD),
        in_specs=[
            pl.BlockSpec((8, D), lambda l, j: (0, 0)),
            pl.BlockSpec((None, D, R), lambda l, j: (l, 0, 0)),
            pl.BlockSpec((None, R, D), lambda l, j: (l, 0, j)),
            pl.BlockSpec((None, 1, D), lambda l, j: (l, 0, j)),
        ],
        out_specs=pl.BlockSpec((None, 8, D), lambda l, j: (l, 0, j)),
        compiler_params=_params(("parallel", "parallel")),
        name="adaln",
    )(cc, w_down, w_up, b.reshape(L, 1, N_MOD * D))


def _pick_row(ref, is_ctx):
    return jnp.where(is_ctx, ref[1:2, :], ref[0:1, :])


def _modulate_kernel(x_ref, sh_ref, sc_ref, u_ref, *, n_lat_tiles):
    is_ctx = pl.program_id(0) >= n_lat_tiles
    u = x_ref[...] * (1.0 + _pick_row(sc_ref, is_ctx)) + _pick_row(sh_ref, is_ctx)
    u_ref[...] = u.astype(BF16)


def _modulate(x, mod, l, k_sh, k_sc, n_lat, tm):
    M, D = x.shape
    return pl.pallas_call(
        functools.partial(_modulate_kernel, n_lat_tiles=n_lat // tm),
        out_shape=jax.ShapeDtypeStruct((M, D), BF16),
        grid=(M // tm,),
        in_specs=[
            pl.BlockSpec((tm, D), lambda i: (i, 0)),
            pl.BlockSpec((None, 8, D), lambda i: (l, 0, k_sh)),
            pl.BlockSpec((None, 8, D), lambda i: (l, 0, k_sc)),
        ],
        out_specs=pl.BlockSpec((tm, D), lambda i: (i, 0)),
        compiler_params=_params(("parallel",)),
        name="modulate",
    )(x, mod, mod)


def _route(logits):
    lane = lax.broadcasted_iota(jnp.int32, logits.shape, 1).astype(F32)
    big = float(ROUTER_LANES)
    gl = jnp.where(lane < N_GROUPS, logits, NEG)
    gmax = jnp.max(gl, axis=-1, keepdims=True)
    gidx = jnp.min(jnp.where(gl == gmax, lane, big), axis=-1, keepdims=True)
    p_group = 1.0 / jnp.sum(jnp.exp(gl - gmax), axis=-1, keepdims=True)
    e = lane - N_GROUPS
    in_group = (e >= gidx * EXP_PER_GROUP) & (e < (gidx + 1) * EXP_PER_GROUP)
    el = jnp.where(in_group, logits, NEG)
    t1 = jnp.max(el, axis=-1, keepdims=True)
    i1 = jnp.min(jnp.where(el == t1, lane, big), axis=-1, keepdims=True)
    el2 = jnp.where(lane == i1, NEG, el)
    t2 = jnp.max(el2, axis=-1, keepdims=True)
    i2 = jnp.min(jnp.where(el2 == t2, lane, big), axis=-1, keepdims=True)
    e2 = jnp.exp(t2 - t1)
    w1 = p_group / (1.0 + e2)
    w2 = p_group * e2 / (1.0 + e2)
    return jnp.where(lane == i1, w1, 0.0) + jnp.where(lane == i2, w2, 0.0)


def _ln_kernel(*refs, n_lat_tiles, alpha, with_u, with_router):
    x_ref, y_ref, gate_ref, lng_ref, lnb_ref = refs[:5]
    pos = 5
    if with_u:
        sh_ref, sc_ref = refs[pos:pos + 2]
        pos += 2
    if with_router:
        wr_ref, br_ref = refs[pos:pos + 2]
        pos += 2
    xo_ref = refs[pos]
    pos += 1
    is_ctx = pl.program_id(0) >= n_lat_tiles
    z = alpha * x_ref[...] + _pick_row(gate_ref, is_ctx) * y_ref[...].astype(F32)
    mu = jnp.mean(z, axis=-1, keepdims=True)
    zc = z - mu
    var = jnp.mean(zc * zc, axis=-1, keepdims=True)
    xn = zc * lax.rsqrt(var + LN_EPS) * lng_ref[...] + lnb_ref[...]
    xo_ref[...] = xn
    if with_u:
        u_ref = refs[pos]
        pos += 1
        u = xn * (1.0 + _pick_row(sc_ref, is_ctx)) + _pick_row(sh_ref, is_ctx)
        u_ref[...] = u.astype(BF16)
        if with_router:
            cw_ref = refs[pos]
            logits = jnp.dot(u, wr_ref[...], precision=HIGHEST, preferred_element_type=F32) + br_ref[...]
            cw_ref[...] = _route(logits)


def _residual_ln(x, y, mod, l, k_gate, ln_g, ln_b, *, n_lat, tm, alpha, rows, nxt=None, router=None):
    M, D = x.shape
    L = ln_g.shape[0]
    in_specs = [
        pl.BlockSpec((tm, D), lambda i: (i, 0)),
        pl.BlockSpec((tm, D), lambda i: (i, 0)),
        pl.BlockSpec((None, 8, D), lambda i: (l, 0, k_gate)),
        pl.BlockSpec((None, 1, D), lambda i: (l, 0, 0)),
        pl.BlockSpec((None, 1, D), lambda i: (l, 0, 0)),
    ]
    args = [x, y, mod, ln_g.reshape(L, 1, D), ln_b.reshape(L, 1, D)]
    out_shape = [jax.ShapeDtypeStruct((rows, D), F32)]
    out_specs = [pl.BlockSpec((tm, D), lambda i: (i, 0))]
    if nxt is not None:
        ln, k_sh, k_sc = nxt
        in_specs += [pl.BlockSpec((None, 8, D), lambda i: (ln, 0, k_sh)),
                     pl.BlockSpec((None, 8, D), lambda i: (ln, 0, k_sc))]
        args += [mod, mod]
        out_shape.append(jax.ShapeDtypeStruct((rows, D), BF16))
        out_specs.append(pl.BlockSpec((tm, D), lambda i: (i, 0)))
    if router is not None:
        wr, br = router
        in_specs += [pl.BlockSpec((None, D, ROUTER_LANES), lambda i: (l, 0, 0)),
                     pl.BlockSpec((None, 1, ROUTER_LANES), lambda i: (l, 0, 0))]
        args += [wr, br]
        out_shape.append(jax.ShapeDtypeStruct((rows, ROUTER_LANES), F32))
        out_specs.append(pl.BlockSpec((tm, ROUTER_LANES), lambda i: (i, 0)))
    return pl.pallas_call(
        functools.partial(_ln_kernel, n_lat_tiles=n_lat // tm, alpha=alpha,
                          with_u=nxt is not None, with_router=router is not None),
        out_shape=out_shape,
        grid=(rows // tm,),
        in_specs=in_specs,
        out_specs=out_specs,
        compiler_params=_params(("parallel",)),
        name="residual_ln",
    )(*args)


def _mm_kernel(a_ref, b_ref, o_ref):
    o_ref[...] = jnp.dot(a_ref[...], b_ref[...], preferred_element_type=F32).astype(o_ref.dtype)


def _matmul(a, b, l, tm, tn):
    M, K = a.shape
    N = b.shape[-1]
    return pl.pallas_call(
        _mm_kernel,
        out_shape=jax.ShapeDtypeStruct((M, N), BF16),
        grid=(M // tm, N // tn),
        in_specs=[pl.BlockSpec((tm, K), lambda i, j: (i, 0)),
                  pl.BlockSpec((None, K, tn), lambda i, j: (l, 0, j))],
        out_specs=pl.BlockSpec((tm, tn), lambda i, j: (i, j)),
        compiler_params=_params(("parallel", "arbitrary")),
        name="matmul",
    )(a, b)


def _local_kernel(prev_ref, cur_ref, next_ref, dw_ref, dwb_ref, lng_ref, lnb_ref, o_ref,
                  glu_s, z_s, y_s, *, T, CW, GW, n_lat, n_ctx):
    s = pl.program_id(0) * T
    in_ctx = s >= n_lat
    seq_start = jnp.where(in_ctx, n_lat, 0)
    seq_len = jnp.where(in_ctx, n_ctx, n_lat)
    keep_prev = jnp.where(s == seq_start, 0.0, 1.0)
    keep_next = jnp.where(s + T == seq_start + seq_len, 0.0, 1.0)

    def stage(ref, row0, rows, keep):
        blk = ref[...]
        a = blk[:, :CW].astype(F32)
        g = blk[:, CW:2 * CW].astype(F32)
        z = blk[:, 2 * CW:3 * CW].astype(F32)
        glu = a * _sigmoid(g)
        if keep is not None:
            glu = glu * keep
            z = z * keep
        glu_s[row0:row0 + rows, :] = glu
        z_s[row0:row0 + rows, :] = z

    stage(prev_ref, 0, HALO, keep_prev)
    stage(cur_ref, HALO, T, None)
    stage(next_ref, HALO + T, HALO, keep_next)

    RC = min(T, 128)
    base = HALO - CONV_K // 2
    for rc in range(T // RC):
        for lc in range(CW // LANES):
            cols = slice(lc * LANES, (lc + 1) * LANES)
            acc = jnp.zeros((RC, LANES), F32)
            for j in range(CONV_K):
                r0 = rc * RC + base + j
                acc = acc + glu_s[r0:r0 + RC, cols] * dw_ref[j:j + 1, cols]
            y_s[rc * RC:(rc + 1) * RC, cols] = acc + dwb_ref[:, cols]

    y = y_s[...]
    mu = jnp.mean(y, axis=-1, keepdims=True)
    yc = y - mu
    var = jnp.mean(yc * yc, axis=-1, keepdims=True)
    hn = yc * lax.rsqrt(var + LN_EPS) * lng_ref[...] + lnb_ref[...]
    o_ref[:, :CW] = (hn * _sigmoid(hn)).astype(BF16)

    pos = lax.broadcasted_iota(jnp.int32, (T, 1), 0) + (s - seq_start)
    for gi, w in enumerate(POOL_WINDOWS):
        cols = slice(gi * GW, (gi + 1) * GW)
        tot = jnp.zeros((T, GW), F32)
        for o in range(-(w // 2), w - w // 2):
            tot = tot + z_s[HALO + o:HALO + o + T, cols]
        cnt = jnp.minimum(pos + (w - w // 2), seq_len) - jnp.maximum(pos - w // 2, 0)
        d = tot / cnt.astype(F32) - z_s[HALO:HALO + T, cols]
        o_ref[:, CW + gi * GW:CW + (gi + 1) * GW] = d.astype(BF16)


def _local_mix(p, conv_dw, conv_dw_b, conv_ln_g, conv_ln_b, l, *, n_lat, n_ctx, T, CW):
    M = p.shape[0]
    L = conv_dw.shape[0]
    GW = CW // len(POOL_WINDOWS)
    hb = T // HALO
    n_hblk = M // HALO
    return pl.pallas_call(
        functools.partial(_local_kernel, T=T, CW=CW, GW=GW, n_lat=n_lat, n_ctx=n_ctx),
        out_shape=jax.ShapeDtypeStruct((M, 2 * CW), BF16),
        grid=(M // T,),
        in_specs=[
            pl.BlockSpec((HALO, 3 * CW), lambda i: (jnp.maximum(i * hb - 1, 0), 0)),
            pl.BlockSpec((T, 3 * CW), lambda i: (i, 0)),
            pl.BlockSpec((HALO, 3 * CW), lambda i: (jnp.minimum((i + 1) * hb, n_hblk - 1), 0)),
            pl.BlockSpec((None, CONV_K, CW), lambda i: (l, 0, 0)),
            pl.BlockSpec((None, 1, CW), lambda i: (l, 0, 0)),
            pl.BlockSpec((None, 1, CW), lambda i: (l, 0, 0)),
            pl.BlockSpec((None, 1, CW), lambda i: (l, 0, 0)),
        ],
        out_specs=pl.BlockSpec((T, 2 * CW), lambda i: (i, 0)),
        scratch_shapes=[pltpu.VMEM((T + 2 * HALO, CW), F32),
                        pltpu.VMEM((T + 2 * HALO, CW), F32),
                        pltpu.VMEM((T, CW), F32)],
        compiler_params=_params(("parallel",)),
        name="local_mix",
    )(p, p, p, conv_dw, conv_dw_b.reshape(L, 1, CW), conv_ln_g.reshape(L, 1, CW),
      conv_ln_b.reshape(L, 1, CW))


def _rope_kernel(x_ref, cos_ref, sin_ref, perm_ref, o_ref, *, CW):
    cos = cos_ref[...]
    sin = sin_ref[...]
    perm = perm_ref[...]
    for h in range(CW // HEAD_DIM):
        cols = slice(h * HEAD_DIM, (h + 1) * HEAD_DIM)
        xb = x_ref[:, cols]
        partner = jnp.dot(xb, perm, preferred_element_type=F32)
        o_ref[:, cols] = (xb.astype(F32) * cos + partner * sin).astype(BF16)


def _rope_tables(n_lat):
    t = jnp.arange(n_lat)
    r = (t // GRID_W).astype(F32)
    col = (t % GRID_W).astype(F32)
    axis_dim = HEAD_DIM // 2
    inv = ROPE_THETA ** (-jnp.arange(0, axis_dim, 2, dtype=F32) / axis_dim)
    ang_r = r[:, None] * inv[None, :]
    ang_c = col[:, None] * inv[None, :]
    cos = jnp.concatenate([jnp.cos(ang_r), jnp.cos(ang_r), jnp.cos(ang_c), jnp.cos(ang_c)], axis=-1)
    sin = jnp.concatenate([-jnp.sin(ang_r), jnp.sin(ang_r), -jnp.sin(ang_c), jnp.sin(ang_c)], axis=-1)
    q = axis_dim // 2
    i = jnp.arange(HEAD_DIM)
    src = jnp.where((i % axis_dim) < q, i + q, i - q)
    perm = (jnp.arange(HEAD_DIM)[:, None] == src[None, :]).astype(BF16)
    return cos, sin, perm


def _rope(p, cos, sin, perm, *, n_lat, T, CW, q_blk):
    return pl.pallas_call(
        functools.partial(_rope_kernel, CW=CW),
        out_shape=jax.ShapeDtypeStruct((n_lat, 4 * CW), BF16),
        grid=(n_lat // T, 4),
        in_specs=[
            pl.BlockSpec((T, CW), lambda i, j: (i, q_blk + j)),
            pl.BlockSpec((T, HEAD_DIM), lambda i, j: (i, 0)),
            pl.BlockSpec((T, HEAD_DIM), lambda i, j: (i, 0)),
            pl.BlockSpec((HEAD_DIM, HEAD_DIM), lambda i, j: (0, 0)),
        ],
        out_specs=pl.BlockSpec((T, CW), lambda i, j: (i, j)),
        compiler_params=_params(("parallel", "arbitrary")),
        name="rope",
    )(p, cos, sin, perm)


def _na_bias_tables(rpb):
    W = GRID_W
    w = jnp.arange(W)
    c0 = jnp.clip(w - NA_KW // 2, 0, W - NA_KW)
    kc = jnp.arange(W)
    dc = kc[None, :] - w[:, None] + (NA_KW - 1)
    valid = (kc[None, :] >= c0[:, None]) & (kc[None, :] < c0[:, None] + NA_KW)
    onehot = (jnp.arange(2 * NA_KW - 1)[:, None, None] == dc[None, :, :]) & valid[None]
    bt = jnp.einsum('lhrd,dwk->lhrwk', rpb, onehot.astype(F32), precision=HIGHEST)
    bt = jnp.where(valid[None, None, None], bt, NEG)
    half = NA_KH // 2
    dr0s = [NA_KH - 1 - half] + [NA_KH - 1 - r for r in range(half)] + [half - 2 - k for k in range(half - 1)]
    tabs = []
    for dr0 in dr0s:
        t = bt[:, :, dr0:dr0 + NA_KH]
        t = jnp.transpose(t, (0, 1, 3, 2, 4)).reshape(t.shape[0], t.shape[1], W, NA_KH * W)
        tabs.append(t)
    return jnp.stack(tabs, axis=1)


def _softmax_pv(s, sc, v, vc):
    m = jnp.maximum(jnp.max(s, axis=-1, keepdims=True), jnp.max(sc, axis=-1, keepdims=True))
    e = jnp.exp(s - m)
    ec = jnp.exp(sc - m)
    den = jnp.sum(e, axis=-1, keepdims=True) + jnp.sum(ec, axis=-1, keepdims=True)
    o = (jnp.dot(e.astype(BF16), v, preferred_element_type=F32)
         + jnp.dot(ec.astype(BF16), vc, preferred_element_type=F32))
    return o / den


def _dot_nt(a, b):
    return lax.dot_general(a, b, (((1,), (1,)), ((), ())), preferred_element_type=F32)


def _na_kernel(qr_ref, qp_ref, kp_ref, kcur_ref, kn_ref, vp_ref, vcur_ref, vn_ref, kc_ref, vc_ref,
               bias_ref, o_ref, kbuf, vbuf, *, HG, rows, scale):
    rb = pl.program_id(1)
    blk = NA_KH * GRID_W
    kbuf[0:blk, :] = kp_ref[...]
    kbuf[blk:2 * blk, :] = kcur_ref[...]
    kbuf[2 * blk:3 * blk, :] = kn_ref[...]
    vbuf[0:blk, :] = vp_ref[...]
    vbuf[blk:2 * blk, :] = vcur_ref[...]
    vbuf[2 * blk:3 * blk, :] = vn_ref[...]
    half = NA_KH // 2

    def body(a, carry):
        r = rb * NA_KH + a
        r0 = jnp.clip(r - half, 0, rows - NA_KH)
        koff = pl.multiple_of((r0 - rb * NA_KH + NA_KH) * GRID_W, GRID_W)
        qoff = pl.multiple_of(a * GRID_W, GRID_W)
        cls = jnp.where(r < half, 1 + r, jnp.where(r > rows - half, r - (rows - half + 1) + half + 1, 0))
        for h in range(HG):
            cols = slice(h * HEAD_DIM, (h + 1) * HEAD_DIM)
            q = qr_ref[pl.ds(qoff, GRID_W), cols]
            qp = qp_ref[pl.ds(qoff, GRID_W), cols]
            k = kbuf[pl.ds(koff, blk), cols]
            v = vbuf[pl.ds(koff, blk), cols]
            s = _dot_nt(q, k) * scale + bias_ref[cls, h]
            sc = _dot_nt(qp, kc_ref[:, cols]) * scale
            o = _softmax_pv(s, sc, v, vc_ref[:, cols])
            o_ref[pl.ds(qoff, GRID_W), cols] = o.astype(BF16)
        return carry

    lax.fori_loop(0, NA_KH, body, 0)


def _na(p, qk, bias, l, *, n_lat, n_ctx, CW, HG):
    M = p.shape[0]
    NA_W = 2 * CW
    heads = NA_W // HEAD_DIM
    rows = n_lat // GRID_W
    blk = NA_KH * GRID_W
    nblk = n_lat // blk
    bw = HG * HEAD_DIM
    q_c, k_c, v_c = 3 * CW // bw, 5 * CW // bw, 7 * CW // bw
    kr_c = NA_W // bw
    ctx_blk = n_lat // n_ctx
    prev = lambda g, r: jnp.maximum(r - 1, 0)
    nxt = lambda g, r: jnp.minimum(r + 1, nblk - 1)
    return pl.pallas_call(
        functools.partial(_na_kernel, HG=HG, rows=rows, scale=HEAD_DIM ** -0.5),
        out_shape=jax.ShapeDtypeStruct((M, NA_W), BF16),
        grid=(heads // HG, nblk),
        in_specs=[
            pl.BlockSpec((blk, bw), lambda g, r: (r, g)),
            pl.BlockSpec((blk, bw), lambda g, r: (r, q_c + g)),
            pl.BlockSpec((blk, bw), lambda g, r: (prev(g, r), kr_c + g)),
            pl.BlockSpec((blk, bw), lambda g, r: (r, kr_c + g)),
            pl.BlockSpec((blk, bw), lambda g, r: (nxt(g, r), kr_c + g)),
            pl.BlockSpec((blk, bw), lambda g, r: (prev(g, r), v_c + g)),
            pl.BlockSpec((blk, bw), lambda g, r: (r, v_c + g)),
            pl.BlockSpec((blk, bw), lambda g, r: (nxt(g, r), v_c + g)),
            pl.BlockSpec((n_ctx, bw), lambda g, r: (ctx_blk, k_c + g)),
            pl.BlockSpec((n_ctx, bw), lambda g, r: (ctx_blk, v_c + g)),
            pl.BlockSpec((None, 8, HG, GRID_W, blk), lambda g, r: (l, 0, g, 0, 0)),
        ],
        out_specs=pl.BlockSpec((blk, bw), lambda g, r: (r, g)),
        scratch_shapes=[pltpu.VMEM((3 * blk, bw), BF16), pltpu.VMEM((3 * blk, bw), BF16)],
        compiler_params=_params(("parallel", "arbitrary")),
        name="neighbourhood_attention",
    )(qk, p, qk, qk, qk, p, p, p, p, p, bias)


def _ctx_attn_kernel(q_ref, k_ref, v_ref, na_ref, o_ref, *, HG, scale):
    del na_ref
    for h in range(HG):
        cols = slice(h * HEAD_DIM, (h + 1) * HEAD_DIM)
        s = _dot_nt(q_ref[:, cols], k_ref[:, cols]) * scale
        m = jnp.max(s, axis=-1, keepdims=True)
        e = jnp.exp(s - m)
        o = jnp.dot(e.astype(BF16), v_ref[:, cols], preferred_element_type=F32)
        o_ref[:, cols] = (o / jnp.sum(e, axis=-1, keepdims=True)).astype(BF16)


def _ctx_attn(p, na, *, n_lat, n_ctx, CW, HG):
    NA_W = 2 * CW
    heads = NA_W // HEAD_DIM
    bw = HG * HEAD_DIM
    q_c, k_c, v_c = 3 * CW // bw, 5 * CW // bw, 7 * CW // bw
    ctx_blk = n_lat // n_ctx
    return pl.pallas_call(
        functools.partial(_ctx_attn_kernel, HG=HG, scale=HEAD_DIM ** -0.5),
        out_shape=jax.ShapeDtypeStruct(na.shape, na.dtype),
        grid=(heads // HG,),
        in_specs=[
            pl.BlockSpec((n_ctx, bw), lambda g: (ctx_blk, q_c + g)),
            pl.BlockSpec((n_ctx, bw), lambda g: (ctx_blk, k_c + g)),
            pl.BlockSpec((n_ctx, bw), lambda g: (ctx_blk, v_c + g)),
            pl.BlockSpec(memory_space=pl.ANY),
        ],
        out_specs=pl.BlockSpec((n_ctx, bw), lambda g: (ctx_blk, g)),
        input_output_aliases={3: 0},
        compiler_params=_params(("parallel",)),
        name="context_attention",
    )(p, p, p, na)


def _merge_kernel(h_ref, d_ref, na_ref, g0_ref, g1_ref, g2_ref, wc_ref, wp_ref, wn_ref, ps_ref, o_ref):
    yc = jnp.dot(h_ref[...], wc_ref[...], preferred_element_type=F32)
    yp = jnp.dot(d_ref[...], wp_ref[...], preferred_element_type=F32) * ps_ref[...]
    yn = jnp.dot(na_ref[...], wn_ref[...], preferred_element_type=F32)
    merged = (_sigmoid(g0_ref[...].astype(F32)) * yc + _sigmoid(g1_ref[...].astype(F32)) * yp
              + _sigmoid(g2_ref[...].astype(F32)) * yn)
    o_ref[...] = merged.astype(BF16)


def _merge(hd, na, p, w_conv_out, pool_w, w_na_out, pool_scale, l, *, CW, tm):
    M = hd.shape[0]
    D = 4 * CW
    L = pool_scale.shape[0]
    G = len(POOL_WINDOWS)
    GW = CW // G
    gate_blk = 9
    return pl.pallas_call(
        _merge_kernel,
        out_shape=jax.ShapeDtypeStruct((M, D), BF16),
        grid=(M // tm, G),
        in_specs=[
            pl.BlockSpec((tm, CW), lambda i, j: (i, 0)),
            pl.BlockSpec((tm, GW), lambda i, j: (i, G + j)),
            pl.BlockSpec((tm, 2 * CW), lambda i, j: (i, 0)),
            pl.BlockSpec((tm, CW), lambda i, j: (i, gate_blk + j)),
            pl.BlockSpec((tm, CW), lambda i, j: (i, gate_blk + G + j)),
            pl.BlockSpec((tm, CW), lambda i, j: (i, gate_blk + 2 * G + j)),
            pl.BlockSpec((None, CW, CW), lambda i, j: (l, 0, j)),
            pl.BlockSpec((None, None, GW, CW), lambda i, j: (l, j, 0, 0)),
            pl.BlockSpec((None, 2 * CW, CW), lambda i, j: (l, 0, j)),
            pl.BlockSpec((None, 1, CW), lambda i, j: (l, 0, j)),
        ],
        out_specs=pl.BlockSpec((tm, CW), lambda i, j: (i, j)),
        compiler_params=_params(("parallel", "arbitrary")),
        name="merge_branches",
    )(hd, hd, na, p, p, p, w_conv_out, pool_w, w_na_out, pool_scale.reshape(L, 1, D))


def _moe_kernel(u_ref, cw_ref, wgu_ref, wd_ref, ex_ref, o_ref, acc_ref, *, EH):
    c = pl.program_id(1)

    @pl.when(c == 0)
    def _():
        acc_ref[...] = jnp.zeros_like(acc_ref)

    gu = jnp.dot(u_ref[...], wgu_ref[...], preferred_element_type=F32)
    g = gu[:, :EH]
    up = gu[:, EH:]
    cwx = jnp.dot(cw_ref[...].astype(BF16), ex_ref[...], preferred_element_type=F32)
    h = (g * _sigmoid(g)) * up * cwx
    acc_ref[...] += jnp.dot(h.astype(BF16), wd_ref[...], preferred_element_type=F32)

    @pl.when(c == pl.num_programs(1) - 1)
    def _():
        o_ref[...] = acc_ref[...].astype(BF16)


def _moe(u, cw, wgu, wd, ex, l, *, tm):
    M, D = u.shape
    NCH = wgu.shape[1]
    EH = wd.shape[2]
    return pl.pallas_call(
        functools.partial(_moe_kernel, EH=EH),
        out_shape=jax.ShapeDtypeStruct((M, D), BF16),
        grid=(M // tm, NCH),
        in_specs=[
            pl.BlockSpec((tm, D), lambda i, c: (i, 0)),
            pl.BlockSpec((tm, ROUTER_LANES), lambda i, c: (i, 0)),
            pl.BlockSpec((None, None, D, 2 * EH), lambda i, c: (l, c, 0, 0)),
            pl.BlockSpec((None, None, EH, D), lambda i, c: (l, c, 0, 0)),
            pl.BlockSpec((None, ROUTER_LANES, EH), lambda i, c: (c, 0, 0)),
        ],
        out_specs=pl.BlockSpec((tm, D), lambda i, c: (i, 0)),
        scratch_shapes=[pltpu.VMEM((tm, D), F32)],
        compiler_params=_params(("parallel", "arbitrary")),
        name="moe_experts",
    )(u, cw, wgu, wd, ex)


def kernel(x, c, ctx, c_ctx, ada_w_down, ada_w_up, ada_b, w_in, conv_dw, conv_dw_b, conv_ln_g,
           conv_ln_b, w_conv_out, pool_w, pool_scale, na_rpb, w_na_out, w_out, ln1_g, ln1_b,
           router_g_w, router_g_b, router_e_w, router_e_b, moe_w_gate, moe_w_up, moe_w_down,
           ln2_g, ln2_b):
    B, n_lat, D = x.shape
    n_ctx = ctx.shape[1]
    L = w_in.shape[0]
    assert B == 1 and D % 16 == 0
    CW = D // 4
    M = n_lat + n_ctx
    rows = n_lat // GRID_W
    H = moe_w_gate.shape[-1]
    alpha = (2 * L) ** 0.25
    assert n_lat % (NA_KH * GRID_W) == 0 and rows >= NA_KH + 2 * (NA_KH // 2) and n_lat % n_ctx == 0
    assert (CW // len(POOL_WINDOWS)) % 8 == 0 and CW % HEAD_DIM == 0

    T = _pick(n_ctx, (256, 128, 64))
    tm_mm = _pick(M, (1280, 640, 320, 256, 128))
    tn_mm = lambda n: _pick(n, (512, 384, 256, 128))
    tm_mid = _pick(M, (640, 320, 256, 128))
    HG = 2 if (2 * CW // HEAD_DIM) % 2 == 0 else 1

    cc = jnp.zeros((8, D), F32).at[0].set(c[0]).at[1].set(c_ctx)
    w_in_b = w_in.astype(BF16)
    w_conv_out_b = w_conv_out.astype(BF16)
    pool_w_b = pool_w.astype(BF16)
    w_na_out_b = w_na_out.astype(BF16)
    w_out_b = w_out.astype(BF16)
    EC = EXPERTS_PER_CHUNK
    NCH = N_EXPERTS // EC

    def chunked(w):
        return jnp.transpose(w.astype(BF16).reshape(L, NCH, EC, D, H), (0, 1, 3, 2, 4)).reshape(L, NCH, D, EC * H)

    wgu = jnp.concatenate([chunked(moe_w_gate), chunked(moe_w_up)], axis=-1)
    wd = moe_w_down.astype(BF16).reshape(L, NCH, EC * H, D)
    lane = jnp.arange(ROUTER_LANES)[None, :, None]
    col_expert = (jnp.arange(NCH)[:, None, None] * EC + jnp.arange(EC * H)[None, None, :] // H)
    ex = (lane == N_GROUPS + col_expert).astype(BF16)
    pad = ROUTER_LANES - N_GROUPS - N_EXPERTS
    wr = jnp.concatenate([router_g_w, router_e_w, jnp.zeros((L, D, pad), F32)], axis=-1)
    br = jnp.concatenate([router_g_b, router_e_b, jnp.zeros((L, pad), F32)], axis=-1).reshape(L, 1, ROUTER_LANES)
    bias = _na_bias_tables(na_rpb)
    cos, sin, perm = _rope_tables(n_lat)

    mod = _adaln(cc, ada_w_down, ada_w_up, ada_b)
    xs = jnp.concatenate([x[0], ctx[0]], axis=0)
    u = _modulate(xs, mod, 0, 0, 1, n_lat, T)

    for l in range(L):
        last = l == L - 1
        p = _matmul(u, w_in_b, l, tm_mm, tn_mm(w_in.shape[-1]))
        hd = _local_mix(p, conv_dw, conv_dw_b, conv_ln_g, conv_ln_b, l, n_lat=n_lat, n_ctx=n_ctx, T=T, CW=CW)
        qk = _rope(p, cos, sin, perm, n_lat=n_lat, T=T, CW=CW, q_blk=3)
        na = _na(p, qk, bias, l, n_lat=n_lat, n_ctx=n_ctx, CW=CW, HG=HG)
        na = _ctx_attn(p, na, n_lat=n_lat, n_ctx=n_ctx, CW=CW, HG=HG)
        merged = _merge(hd, na, p, w_conv_out_b, pool_w_b, w_na_out_b, pool_scale, l, CW=CW, tm=tm_mid)
        y = _matmul(merged, w_out_b, l, tm_mm, tn_mm(D))
        xs, u2, cw = _residual_ln(xs, y, mod, l, 2, ln1_g, ln1_b, n_lat=n_lat, tm=T, alpha=alpha, rows=M,
                                  nxt=(l, 3, 4), router=(wr, br))
        f = _moe(u2, cw, wgu, wd, ex, l, tm=tm_mid)
        if last:
            (xs,) = _residual_ln(xs, f, mod, l, 5, ln2_g, ln2_b, n_lat=n_lat, tm=T, alpha=alpha, rows=n_lat)
        else:
            xs, u = _residual_ln(xs, f, mod, l, 5, ln2_g, ln2_b, n_lat=n_lat, tm=T, alpha=alpha, rows=M,
                                 nxt=(l + 1, 0, 1))
    return xs[None]
```

```python
import functools

import jax
import jax.numpy as jnp
from jax import lax
from jax.experimental import pallas as pl
from jax.experimental.pallas import tpu as pltpu

F32 = jnp.float32
BF16 = jnp.bfloat16
HIGHEST = lax.Precision.HIGHEST

GRID_W = 64
CONV_K = 31
POOL_WINDOWS = (2, 4, 8, 16)
HEAD_DIM = 128
NA_KH = 8
NA_KW = 16
N_GROUPS = 4
EXP_PER_GROUP = 8
N_EXPERTS = N_GROUPS * EXP_PER_GROUP
N_MOD = 6
ROPE_THETA = 10000.0
LN_EPS = 1e-5
NEG = -1e30

VMEM_LIMIT_BYTES = 56 * 1024 * 1024
LANES = 128
HALO = 16
ROUTER_LANES = 128
EXPERTS_PER_CHUNK = 2
NA_QROWS = NA_KH // 2
MOE_SUB_ROWS = 256


def _params(sem):
    return pltpu.CompilerParams(dimension_semantics=sem, vmem_limit_bytes=VMEM_LIMIT_BYTES)


def _pick(n, cands):
    for c in cands:
        if n % c == 0:
            return c
    raise ValueError(f"no tile in {cands} divides {n}")


def _sigmoid(x):
    return 1.0 / (1.0 + jnp.exp(-x))


def _adaln_kernel(c_ref, wd_ref, wu_ref, b_ref, o_ref):
    c = c_ref[...]
    s = c * _sigmoid(c)
    t = jnp.dot(s, wd_ref[...], precision=HIGHEST, preferred_element_type=F32)
    o_ref[...] = jnp.dot(t, wu_ref[...], precision=HIGHEST, preferred_element_type=F32) + b_ref[...]


def _adaln(cc, w_down, w_up, b):
    L, D, R = w_down.shape
    return pl.pallas_call(
        _adaln_kernel,
        out_shape=jax.ShapeDtypeStruct((L, 8, N_MOD * D), F32),
        grid=(L, N_MOD),
        in_specs=[
            pl.BlockSpec((8, D), lambda l, j: (0, 0)),
            pl.BlockSpec((None, D, R), lambda l, j: (l, 0, 0)),
            pl.BlockSpec((None, R, D), lambda l, j: (l, 0, j)),
            pl.BlockSpec((None, 1, D), lambda l, j: (l, 0, j)),
        ],
        out_specs=pl.BlockSpec((None, 8, D), lambda l, j: (l, 0, j)),
        compiler_params=_params(("parallel", "parallel")),
        name="adaln",
    )(cc, w_down, w_up, b.reshape(L, 1, N_MOD * D))


def _pick_row(ref, is_ctx):
    return jnp.where(is_ctx, ref[1:2, :], ref[0:1, :])


def _modulate_kernel(x_ref, sh_ref, sc_ref, u_ref, *, n_lat_tiles):
    is_ctx = pl.program_id(0) >= n_lat_tiles
    u = x_ref[...] * (1.0 + _pick_row(sc_ref, is_ctx)) + _pick_row(sh_ref, is_ctx)
    u_ref[...] = u.astype(BF16)


def _modulate(x, mod, l, k_sh, k_sc, n_lat, tm):
    M, D = x.shape
    return pl.pallas_call(
        functools.partial(_modulate_kernel, n_lat_tiles=n_lat // tm),
        out_shape=jax.ShapeDtypeStruct((M, D), BF16),
        grid=(M // tm,),
        in_specs=[
            pl.BlockSpec((tm, D), lambda i: (i, 0)),
            pl.BlockSpec((None, 8, D), lambda i: (l, 0, k_sh)),
            pl.BlockSpec((None, 8, D), lambda i: (l, 0, k_sc)),
        ],
        out_specs=pl.BlockSpec((tm, D), lambda i: (i, 0)),
        compiler_params=_params(("parallel",)),
        name="modulate",
    )(x, mod, mod)


def _route(logits):
    lane = lax.broadcasted_iota(jnp.int32, logits.shape, 1).astype(F32)
    big = float(ROUTER_LANES)
    gl = jnp.where(lane < N_GROUPS, logits, NEG)
    gmax = jnp.max(gl, axis=-1, keepdims=True)
    gidx = jnp.min(jnp.where(gl == gmax, lane, big), axis=-1, keepdims=True)
    p_group = 1.0 / jnp.sum(jnp.exp(gl - gmax), axis=-1, keepdims=True)
    e = lane - N_GROUPS
    in_group = (e >= gidx * EXP_PER_GROUP) & (e < (gidx + 1) * EXP_PER_GROUP)
    el = jnp.where(in_group, logits, NEG)
    t1 = jnp.max(el, axis=-1, keepdims=True)
    i1 = jnp.min(jnp.where(el == t1, lane, big), axis=-1, keepdims=True)
    el2 = jnp.where(lane == i1, NEG, el)
    t2 = jnp.max(el2, axis=-1, keepdims=True)
    i2 = jnp.min(jnp.where(el2 == t2, lane, big), axis=-1, keepdims=True)
    e2 = jnp.exp(t2 - t1)
    w1 = p_group / (1.0 + e2)
    w2 = p_group * e2 / (1.0 + e2)
    cw = jnp.where(lane == i1, w1, 0.0) + jnp.where(lane == i2, w2, 0.0)
    return jnp.where(lane == 0.0, gidx, cw)


def _ln_kernel(*refs, n_lat_tiles, alpha, with_u, with_router):
    x_ref, y_ref, gate_ref, lng_ref, lnb_ref = refs[:5]
    pos = 5
    if with_u:
        sh_ref, sc_ref = refs[pos:pos + 2]
        pos += 2
    if with_router:
        wr_ref, br_ref = refs[pos:pos + 2]
        pos += 2
    xo_ref = refs[pos]
    pos += 1
    is_ctx = pl.program_id(0) >= n_lat_tiles
    z = alpha * x_ref[...] + _pick_row(gate_ref, is_ctx) * y_ref[...].astype(F32)
    mu = jnp.mean(z, axis=-1, keepdims=True)
    zc = z - mu
    var = jnp.mean(zc * zc, axis=-1, keepdims=True)
    xn = zc * lax.rsqrt(var + LN_EPS) * lng_ref[...] + lnb_ref[...]
    xo_ref[...] = xn
    if with_u:
        u_ref = refs[pos]
        pos += 1
        u = xn * (1.0 + _pick_row(sc_ref, is_ctx)) + _pick_row(sh_ref, is_ctx)
        if with_router:
            D = u.shape[-1]
            logits = jnp.dot(u, wr_ref[...], precision=HIGHEST, preferred_element_type=F32) + br_ref[...]
            u_ref[:, :D] = u
            u_ref[:, D:] = _route(logits)
        else:
            u_ref[...] = u.astype(BF16)


def _residual_ln(x, y, mod, l, k_gate, ln_g, ln_b, *, n_lat, tm, alpha, rows, nxt=None, router=None):
    M, D = x.shape
    L = ln_g.shape[0]
    in_specs = [
        pl.BlockSpec((tm, D), lambda i: (i, 0)),
        pl.BlockSpec((tm, D), lambda i: (i, 0)),
        pl.BlockSpec((None, 8, D), lambda i: (l, 0, k_gate)),
        pl.BlockSpec((None, 1, D), lambda i: (l, 0, 0)),
        pl.BlockSpec((None, 1, D), lambda i: (l, 0, 0)),
    ]
    args = [x, y, mod, ln_g.reshape(L, 1, D), ln_b.reshape(L, 1, D)]
    out_shape = [jax.ShapeDtypeStruct((rows, D), F32)]
    out_specs = [pl.BlockSpec((tm, D), lambda i: (i, 0))]
    if nxt is not None:
        ln, k_sh, k_sc = nxt
        in_specs += [pl.BlockSpec((None, 8, D), lambda i: (ln, 0, k_sh)),
                     pl.BlockSpec((None, 8, D), lambda i: (ln, 0, k_sc))]
        args += [mod, mod]
        if router is None:
            out_shape.append(jax.ShapeDtypeStruct((rows, D), BF16))
            out_specs.append(pl.BlockSpec((tm, D), lambda i: (i, 0)))
        else:
            out_shape.append(jax.ShapeDtypeStruct((rows, D + ROUTER_LANES), F32))
            out_specs.append(pl.BlockSpec((tm, D + ROUTER_LANES), lambda i: (i, 0)))
    if router is not None:
        wr, br = router
        in_specs += [pl.BlockSpec((None, D, ROUTER_LANES), lambda i: (l, 0, 0)),
                     pl.BlockSpec((None, 1, ROUTER_LANES), lambda i: (l, 0, 0))]
        args += [wr, br]
    return pl.pallas_call(
        functools.partial(_ln_kernel, n_lat_tiles=n_lat // tm, alpha=alpha,
                          with_u=nxt is not None, with_router=router is not None),
        out_shape=out_shape,
        grid=(rows // tm,),
        in_specs=in_specs,
        out_specs=out_specs,
        compiler_params=_params(("parallel",)),
        name="residual_ln",
    )(*args)


def _mm_kernel(a_ref, b_ref, o_ref):
    o_ref[...] = jnp.dot(a_ref[...], b_ref[...], preferred_element_type=F32).astype(o_ref.dtype)


def _matmul(a, b, l, tm, tn):
    M, K = a.shape
    N = b.shape[-1]
    return pl.pallas_call(
        _mm_kernel,
        out_shape=jax.ShapeDtypeStruct((M, N), BF16),
        grid=(M // tm, N // tn),
        in_specs=[pl.BlockSpec((tm, K), lambda i, j: (i, 0)),
                  pl.BlockSpec((None, K, tn), lambda i, j: (l, 0, j))],
        out_specs=pl.BlockSpec((tm, tn), lambda i, j: (i, j)),
        compiler_params=_params(("parallel", "arbitrary")),
        name="matmul",
    )(a, b)


def _local_kernel(prev_ref, cur_ref, next_ref, dw_ref, dwb_ref, lng_ref, lnb_ref, o_ref,
                  glu_s, z_s, y_s, *, T, CW, GW, n_lat, n_ctx):
    s = pl.program_id(0) * T
    in_ctx = s >= n_lat
    seq_start = jnp.where(in_ctx, n_lat, 0)
    seq_len = jnp.where(in_ctx, n_ctx, n_lat)
    keep_prev = jnp.where(s == seq_start, 0.0, 1.0)
    keep_next = jnp.where(s + T == seq_start + seq_len, 0.0, 1.0)

    def stage(ref, row0, rows, keep):
        blk = ref[...]
        a = blk[:, :CW].astype(F32)
        g = blk[:, CW:2 * CW].astype(F32)
        z = blk[:, 2 * CW:3 * CW].astype(F32)
        glu = a * _sigmoid(g)
        if keep is not None:
            glu = glu * keep
            z = z * keep
        glu_s[row0:row0 + rows, :] = glu
        z_s[row0:row0 + rows, :] = z

    stage(prev_ref, 0, HALO, keep_prev)
    stage(cur_ref, HALO, T, None)
    stage(next_ref, HALO + T, HALO, keep_next)

    RC = min(T, 128)
    base = HALO - CONV_K // 2
    for rc in range(T // RC):
        for lc in range(CW // LANES):
            cols = slice(lc * LANES, (lc + 1) * LANES)
            acc = jnp.zeros((RC, LANES), F32)
            for j in range(CONV_K):
                r0 = rc * RC + base + j
                acc = acc + glu_s[r0:r0 + RC, cols] * dw_ref[j:j + 1, cols]
            y_s[rc * RC:(rc + 1) * RC, cols] = acc + dwb_ref[:, cols]

    y = y_s[...]
    mu = jnp.mean(y, axis=-1, keepdims=True)
    yc = y - mu
    var = jnp.mean(yc * yc, axis=-1, keepdims=True)
    hn = yc * lax.rsqrt(var + LN_EPS) * lng_ref[...] + lnb_ref[...]
    o_ref[:, :CW] = (hn * _sigmoid(hn)).astype(BF16)

    pos = lax.broadcasted_iota(jnp.int32, (T, 1), 0) + (s - seq_start)
    for gi, w in enumerate(POOL_WINDOWS):
        cols = slice(gi * GW, (gi + 1) * GW)
        tot = jnp.zeros((T, GW), F32)
        for o in range(-(w // 2), w - w // 2):
            tot = tot + z_s[HALO + o:HALO + o + T, cols]
        cnt = jnp.minimum(pos + (w - w // 2), seq_len) - jnp.maximum(pos - w // 2, 0)
        d = tot / cnt.astype(F32) - z_s[HALO:HALO + T, cols]
        o_ref[:, CW + gi * GW:CW + (gi + 1) * GW] = d.astype(BF16)


def _local_mix(p, conv_dw, conv_dw_b, conv_ln_g, conv_ln_b, l, *, n_lat, n_ctx, T, CW):
    M = p.shape[0]
    L = conv_dw.shape[0]
    GW = CW // len(POOL_WINDOWS)
    hb = T // HALO
    n_hblk = M // HALO
    return pl.pallas_call(
        functools.partial(_local_kernel, T=T, CW=CW, GW=GW, n_lat=n_lat, n_ctx=n_ctx),
        out_shape=jax.ShapeDtypeStruct((M, 2 * CW), BF16),
        grid=(M // T,),
        in_specs=[
            pl.BlockSpec((HALO, 3 * CW), lambda i: (jnp.maximum(i * hb - 1, 0), 0)),
            pl.BlockSpec((T, 3 * CW), lambda i: (i, 0)),
            pl.BlockSpec((HALO, 3 * CW), lambda i: (jnp.minimum((i + 1) * hb, n_hblk - 1), 0)),
            pl.BlockSpec((None, CONV_K, CW), lambda i: (l, 0, 0)),
            pl.BlockSpec((None, 1, CW), lambda i: (l, 0, 0)),
            pl.BlockSpec((None, 1, CW), lambda i: (l, 0, 0)),
            pl.BlockSpec((None, 1, CW), lambda i: (l, 0, 0)),
        ],
        out_specs=pl.BlockSpec((T, 2 * CW), lambda i: (i, 0)),
        scratch_shapes=[pltpu.VMEM((T + 2 * HALO, CW), F32),
                        pltpu.VMEM((T + 2 * HALO, CW), F32),
                        pltpu.VMEM((T, CW), F32)],
        compiler_params=_params(("parallel",)),
        name="local_mix",
    )(p, p, p, conv_dw, conv_dw_b.reshape(L, 1, CW), conv_ln_g.reshape(L, 1, CW),
      conv_ln_b.reshape(L, 1, CW))


def _rope_kernel(x_ref, cos_ref, sin_ref, perm_ref, o_ref, *, CW):
    cos = cos_ref[...]
    sin = sin_ref[...]
    perm = perm_ref[...]
    for h in range(CW // HEAD_DIM):
        cols = slice(h * HEAD_DIM, (h + 1) * HEAD_DIM)
        xb = x_ref[:, cols]
        partner = jnp.dot(xb, perm, preferred_element_type=F32)
        o_ref[:, cols] = (xb.astype(F32) * cos + partner * sin).astype(BF16)


def _rope_tables(n_lat):
    t = jnp.arange(n_lat)
    r = (t // GRID_W).astype(F32)
    col = (t % GRID_W).astype(F32)
    axis_dim = HEAD_DIM // 2
    inv = ROPE_THETA ** (-jnp.arange(0, axis_dim, 2, dtype=F32) / axis_dim)
    ang_r = r[:, None] * inv[None, :]
    ang_c = col[:, None] * inv[None, :]
    cos = jnp.concatenate([jnp.cos(ang_r), jnp.cos(ang_r), jnp.cos(ang_c), jnp.cos(ang_c)], axis=-1)
    sin = jnp.concatenate([-jnp.sin(ang_r), jnp.sin(ang_r), -jnp.sin(ang_c), jnp.sin(ang_c)], axis=-1)
    q = axis_dim // 2
    i = jnp.arange(HEAD_DIM)
    src = jnp.where((i % axis_dim) < q, i + q, i - q)
    perm = (jnp.arange(HEAD_DIM)[:, None] == src[None, :]).astype(BF16)
    return cos, sin, perm


def _rope(p, cos, sin, perm, *, n_lat, T, CW, q_blk):
    return pl.pallas_call(
        functools.partial(_rope_kernel, CW=CW),
        out_shape=jax.ShapeDtypeStruct((n_lat, 4 * CW), BF16),
        grid=(n_lat // T, 4),
        in_specs=[
            pl.BlockSpec((T, CW), lambda i, j: (i, q_blk + j)),
            pl.BlockSpec((T, HEAD_DIM), lambda i, j: (i, 0)),
            pl.BlockSpec((T, HEAD_DIM), lambda i, j: (i, 0)),
            pl.BlockSpec((HEAD_DIM, HEAD_DIM), lambda i, j: (0, 0)),
        ],
        out_specs=pl.BlockSpec((T, CW), lambda i, j: (i, j)),
        compiler_params=_params(("parallel", "arbitrary")),
        name="rope",
    )(p, cos, sin, perm)


def _na_bias_tables(rpb):
    W = GRID_W
    w = jnp.arange(W)
    c0 = jnp.clip(w - NA_KW // 2, 0, W - NA_KW)
    kc = jnp.arange(W)
    dc = kc[None, :] - w[:, None] + (NA_KW - 1)
    valid = (kc[None, :] >= c0[:, None]) & (kc[None, :] < c0[:, None] + NA_KW)
    onehot = (jnp.arange(2 * NA_KW - 1)[:, None, None] == dc[None, :, :]) & valid[None]
    bt = jnp.einsum('lhrd,dwk->lhrwk', rpb, onehot.astype(F32), precision=HIGHEST)
    bt = jnp.where(valid[None, None, None], bt, NEG)
    QR = NA_QROWS
    L, H = rpb.shape[:2]
    t = jnp.stack([bt[:, :, NA_KH - 1 - QR - a:NA_KH - 1 - QR - a + 3 * QR] for a in range(QR)], axis=2)
    a = jnp.arange(QR)[:, None]
    b = jnp.arange(3 * QR)[None, :]
    valid_rows = jnp.stack([(b - a >= 0) & (b - a < NA_KH),
                            jnp.broadcast_to((b >= QR) & (b < QR + NA_KH), (QR, 3 * QR)),
                            jnp.broadcast_to(b < NA_KH, (QR, 3 * QR))])
    t = jnp.where(valid_rows[None, :, None, :, :, None, None], t[:, None], NEG)
    t = jnp.transpose(t, (0, 1, 2, 3, 5, 4, 6))
    return t.reshape(L, 3, H, QR * W, 3 * QR * W)


def _dot_nt(a, b):
    return lax.dot_general(a, b, (((1,), (1,)), ((), ())), preferred_element_type=F32)


def _na_kernel(qr_ref, qp_ref, kp_ref, kcur_ref, kn_ref, vp_ref, vcur_ref, vn_ref, kc_ref, vc_ref,
               bias_ref, o_ref, *, HG, nblk, scale):
    rb = pl.program_id(1)
    B = NA_QROWS * GRID_W

    @pl.when(rb < nblk)
    def _():
        for h in range(HG):
            cols = slice(h * HEAD_DIM, (h + 1) * HEAD_DIM)
            q = qr_ref[:, cols]
            s = jnp.concatenate([_dot_nt(q, kp_ref[:, cols]), _dot_nt(q, kcur_ref[:, cols]),
                                 _dot_nt(q, kn_ref[:, cols])], axis=1) * scale + bias_ref[h]
            sc = _dot_nt(qp_ref[:, cols], kc_ref[:, cols]) * scale
            m = jnp.maximum(jnp.max(s, axis=-1, keepdims=True), jnp.max(sc, axis=-1, keepdims=True))
            e = jnp.exp(s - m)
            ec = jnp.exp(sc - m)
            den = jnp.sum(e, axis=-1, keepdims=True) + jnp.sum(ec, axis=-1, keepdims=True)
            eb = e.astype(BF16)
            o = (jnp.dot(eb[:, 0:B], vp_ref[:, cols], preferred_element_type=F32)
                 + jnp.dot(eb[:, B:2 * B], vcur_ref[:, cols], preferred_element_type=F32)
                 + jnp.dot(eb[:, 2 * B:3 * B], vn_ref[:, cols], preferred_element_type=F32)
                 + jnp.dot(ec.astype(BF16), vc_ref[:, cols], preferred_element_type=F32))
            o_ref[:, cols] = (o / den).astype(BF16)

    @pl.when(rb == nblk)
    def _():
        for h in range(HG):
            cols = slice(h * HEAD_DIM, (h + 1) * HEAD_DIM)
            s = _dot_nt(qp_ref[:, cols], kc_ref[:, cols]) * scale
            m = jnp.max(s, axis=-1, keepdims=True)
            e = jnp.exp(s - m)
            o = jnp.dot(e.astype(BF16), vc_ref[:, cols], preferred_element_type=F32)
            o_ref[:, cols] = (o / jnp.sum(e, axis=-1, keepdims=True)).astype(BF16)


def _na(p, qk, bias, l, *, n_lat, n_ctx, CW, HG):
    M = p.shape[0]
    NA_W = 2 * CW
    heads = NA_W // HEAD_DIM
    blk = NA_QROWS * GRID_W
    nblk = n_lat // blk
    assert n_ctx == blk and nblk >= 2
    bw = HG * HEAD_DIM
    q_c, k_c, v_c = 3 * CW // bw, 5 * CW // bw, 7 * CW // bw
    kr_c = NA_W // bw
    ctx_blk = n_lat // n_ctx
    cur = lambda r: jnp.minimum(r, nblk - 1)
    prev = lambda r: jnp.maximum(cur(r) - 1, 0)
    nxt = lambda r: jnp.minimum(r + 1, nblk - 1)
    cls = lambda r: jnp.where(r == 0, 1, jnp.where(r >= nblk - 1, 2, 0))
    return pl.pallas_call(
        functools.partial(_na_kernel, HG=HG, nblk=nblk, scale=HEAD_DIM ** -0.5),
        out_shape=jax.ShapeDtypeStruct((M, NA_W), BF16),
        grid=(heads // HG, nblk + 1),
        in_specs=[
            pl.BlockSpec((blk, bw), lambda g, r: (cur(r), g)),
            pl.BlockSpec((blk, bw), lambda g, r: (r, q_c + g)),
            pl.BlockSpec((blk, bw), lambda g, r: (prev(r), kr_c + g)),
            pl.BlockSpec((blk, bw), lambda g, r: (cur(r), kr_c + g)),
            pl.BlockSpec((blk, bw), lambda g, r: (nxt(r), kr_c + g)),
            pl.BlockSpec((blk, bw), lambda g, r: (prev(r), v_c + g)),
            pl.BlockSpec((blk, bw), lambda g, r: (cur(r), v_c + g)),
            pl.BlockSpec((blk, bw), lambda g, r: (nxt(r), v_c + g)),
            pl.BlockSpec((n_ctx, bw), lambda g, r: (ctx_blk, k_c + g)),
            pl.BlockSpec((n_ctx, bw), lambda g, r: (ctx_blk, v_c + g)),
            pl.BlockSpec((None, None, HG, blk, 3 * blk), lambda g, r: (l, cls(r), g, 0, 0)),
        ],
        out_specs=pl.BlockSpec((blk, bw), lambda g, r: (r, g)),
        compiler_params=_params(("parallel", "arbitrary")),
        name="neighbourhood_attention",
    )(qk, p, qk, qk, qk, p, p, p, p, p, bias)


def _merge_kernel(h_ref, d_ref, na_ref, g0_ref, g1_ref, g2_ref, wc_ref, wp_ref, wn_ref, ps_ref, o_ref):
    yc = jnp.dot(h_ref[...], wc_ref[...], preferred_element_type=F32)
    yp = jnp.dot(d_ref[...], wp_ref[...], preferred_element_type=F32) * ps_ref[...]
    yn = jnp.dot(na_ref[...], wn_ref[...], preferred_element_type=F32)
    merged = (_sigmoid(g0_ref[...].astype(F32)) * yc + _sigmoid(g1_ref[...].astype(F32)) * yp
              + _sigmoid(g2_ref[...].astype(F32)) * yn)
    o_ref[...] = merged.astype(BF16)


def _merge(hd, na, p, w_conv_out, pool_w, w_na_out, pool_scale, l, *, CW, tm):
    M = hd.shape[0]
    D = 4 * CW
    L = pool_scale.shape[0]
    G = len(POOL_WINDOWS)
    GW = CW // G
    gate_blk = 9
    return pl.pallas_call(
        _merge_kernel,
        out_shape=jax.ShapeDtypeStruct((M, D), BF16),
        grid=(M // tm, G),
        in_specs=[
            pl.BlockSpec((tm, CW), lambda i, j: (i, 0)),
            pl.BlockSpec((tm, GW), lambda i, j: (i, G + j)),
            pl.BlockSpec((tm, 2 * CW), lambda i, j: (i, 0)),
            pl.BlockSpec((tm, CW), lambda i, j: (i, gate_blk + j)),
            pl.BlockSpec((tm, CW), lambda i, j: (i, gate_blk + G + j)),
            pl.BlockSpec((tm, CW), lambda i, j: (i, gate_blk + 2 * G + j)),
            pl.BlockSpec((None, CW, CW), lambda i, j: (l, 0, j)),
            pl.BlockSpec((None, None, GW, CW), lambda i, j: (l, j, 0, 0)),
            pl.BlockSpec((None, 2 * CW, CW), lambda i, j: (l, 0, j)),
            pl.BlockSpec((None, 1, CW), lambda i, j: (l, 0, j)),
        ],
        out_specs=pl.BlockSpec((tm, CW), lambda i, j: (i, j)),
        compiler_params=_params(("parallel", "arbitrary")),
        name="merge_branches",
    )(hd, hd, na, p, p, p, w_conv_out, pool_w, w_na_out, pool_scale.reshape(L, 1, D))


def _moe_kernel(tg_ref, src_ref, dst_ref, u_hbm, wgu_ref, wd_ref, ex_ref, f_hbm,
                gbuf, ubf, cwb, acc, obuf, sem_in, sem_out, *, tA, D, EH, n_tiles, NC):
    del tg_ref
    t = pl.program_id(0)
    c = pl.program_id(1)

    def start_gather(tile):
        def body(r, carry):
            tok = src_ref[tile * tA + r]
            pltpu.make_async_copy(u_hbm.at[pl.ds(tok, 1)], gbuf.at[pl.ds(r, 1)], sem_in).start()
            return carry
        lax.fori_loop(0, tA, body, 0, unroll=8)

    def start_scatter(tile):
        def body(r, carry):
            tok = dst_ref[tile * tA + r]
            pltpu.make_async_copy(obuf.at[pl.ds(r, 1)], f_hbm.at[pl.ds(tok, 1)], sem_out).start()
            return carry
        lax.fori_loop(0, tA, body, 0, unroll=8)

    def wait_rows(buf, sem):
        pltpu.make_async_copy(buf, buf, sem).wait()

    @pl.when((t == 0) & (c == 0))
    def _():
        start_gather(0)
        acc[...] = jnp.zeros_like(acc)

    @pl.when(c == 0)
    def _():
        wait_rows(gbuf, sem_in)
        ubf[...] = gbuf[:, :D].astype(BF16)
        cwb[...] = gbuf[:, D:].astype(BF16)

        @pl.when(t + 1 < n_tiles)
        def _():
            start_gather(t + 1)

    for sb in range(tA // MOE_SUB_ROWS):
        rows = slice(sb * MOE_SUB_ROWS, (sb + 1) * MOE_SUB_ROWS)
        gu = jnp.dot(ubf[rows, :], wgu_ref[...], preferred_element_type=F32)
        g = gu[:, :EH]
        up = gu[:, EH:]
        cwx = jnp.dot(cwb[rows, :], ex_ref[...], preferred_element_type=F32)
        h = (g * _sigmoid(g)) * up * cwx
        contrib = jnp.dot(h.astype(BF16), wd_ref[...], preferred_element_type=F32)
        acc[rows, :] = jnp.where(c == 0, 0.0, acc[rows, :]) + contrib

    @pl.when(c == NC - 1)
    def _():
        @pl.when(t > 0)
        def _():
            wait_rows(obuf, sem_out)

        obuf[...] = acc[...]
        start_scatter(t)

        @pl.when(t == n_tiles - 1)
        def _():
            wait_rows(obuf, sem_out)


def _moe_plan(group, M, tA, n_tiles):
    onehot = (group[:, None] == jnp.arange(N_GROUPS)[None, :]).astype(jnp.int32)
    rank = jnp.sum((jnp.cumsum(onehot, axis=0) - onehot) * onehot, axis=1)
    tiles_g = (jnp.sum(onehot, axis=0) + tA - 1) // tA
    first_tile = jnp.cumsum(tiles_g) - tiles_g
    pos = first_tile[group] * tA + rank
    tok = jnp.arange(M, dtype=jnp.int32)
    src = jnp.zeros((n_tiles * tA,), jnp.int32).at[pos].set(tok)
    dst = (M + jnp.arange(n_tiles * tA, dtype=jnp.int32) % tA).at[pos].set(tok)
    tile_group = jnp.sum(jnp.arange(n_tiles)[:, None] >= first_tile[None, :], axis=1) - 1
    return tile_group.astype(jnp.int32), src, dst


def _moe(uext, wgu, wd, ex, l, *, tA):
    M = uext.shape[0]
    D = uext.shape[1] - ROUTER_LANES
    EH = wd.shape[2]
    NC = EXP_PER_GROUP // EXPERTS_PER_CHUNK
    n_tiles = (M + N_GROUPS * (tA - 1)) // tA
    tile_group, src, dst = _moe_plan(uext[:, D].astype(jnp.int32), M, tA, n_tiles)
    grid_spec = pltpu.PrefetchScalarGridSpec(
        num_scalar_prefetch=3,
        grid=(n_tiles, NC),
        in_specs=[
            pl.BlockSpec(memory_space=pl.ANY),
            pl.BlockSpec((None, None, D, 2 * EH), lambda t, c, tg, s, d: (l, tg[t] * NC + c, 0, 0)),
            pl.BlockSpec((None, None, EH, D), lambda t, c, tg, s, d: (l, tg[t] * NC + c, 0, 0)),
            pl.BlockSpec((None, ROUTER_LANES, EH), lambda t, c, tg, s, d: (tg[t] * NC + c, 0, 0)),
        ],
        out_specs=pl.BlockSpec(memory_space=pl.ANY),
        scratch_shapes=[
            pltpu.VMEM((tA, D + ROUTER_LANES), F32),
            pltpu.VMEM((tA, D), BF16),
            pltpu.VMEM((tA, ROUTER_LANES), BF16),
            pltpu.VMEM((tA, D), F32),
            pltpu.VMEM((tA, D), F32),
            pltpu.SemaphoreType.DMA(()),
            pltpu.SemaphoreType.DMA(()),
        ],
    )
    return pl.pallas_call(
        functools.partial(_moe_kernel, tA=tA, D=D, EH=EH, n_tiles=n_tiles, NC=NC),
        out_shape=jax.ShapeDtypeStruct((M + tA, D), F32),
        grid_spec=grid_spec,
        compiler_params=_params(("arbitrary", "arbitrary")),
        name="moe_experts",
    )(tile_group, src, dst, uext, wgu, wd, ex)


def kernel(x, c, ctx, c_ctx, ada_w_down, ada_w_up, ada_b, w_in, conv_dw, conv_dw_b, conv_ln_g,
           conv_ln_b, w_conv_out, pool_w, pool_scale, na_rpb, w_na_out, w_out, ln1_g, ln1_b,
           router_g_w, router_g_b, router_e_w, router_e_b, moe_w_gate, moe_w_up, moe_w_down,
           ln2_g, ln2_b):
    B, n_lat, D = x.shape
    n_ctx = ctx.shape[1]
    L = w_in.shape[0]
    assert B == 1 and D % 16 == 0
    CW = D // 4
    M = n_lat + n_ctx
    rows = n_lat // GRID_W
    H = moe_w_gate.shape[-1]
    alpha = (2 * L) ** 0.25
    assert n_lat % (NA_QROWS * GRID_W) == 0 and rows >= NA_KH and n_lat % n_ctx == 0
    assert (CW // len(POOL_WINDOWS)) % 8 == 0 and CW % HEAD_DIM == 0

    T = _pick(n_ctx, (256, 128, 64))
    tm_mm = _pick(M, (1280, 640, 320, 256, 128))
    tn_mm = lambda n: _pick(n, (512, 384, 256, 128))
    tm_mid = _pick(M, (640, 320, 256, 128))
    tA = min(512, T * 2)
    HG = _pick(CW // HEAD_DIM, (4, 2, 1))

    cc = jnp.zeros((8, D), F32).at[0].set(c[0]).at[1].set(c_ctx)
    w_in_b = w_in.astype(BF16)
    w_conv_out_b = w_conv_out.astype(BF16)
    pool_w_b = pool_w.astype(BF16)
    w_na_out_b = w_na_out.astype(BF16)
    w_out_b = w_out.astype(BF16)
    EC = EXPERTS_PER_CHUNK
    NCH = N_EXPERTS // EC

    def chunked(w):
        return jnp.transpose(w.astype(BF16).reshape(L, NCH, EC, D, H), (0, 1, 3, 2, 4)).reshape(L, NCH, D, EC * H)

    wgu = jnp.concatenate([chunked(moe_w_gate), chunked(moe_w_up)], axis=-1)
    wd = moe_w_down.astype(BF16).reshape(L, NCH, EC * H, D)
    lane = jnp.arange(ROUTER_LANES)[None, :, None]
    col_expert = (jnp.arange(NCH)[:, None, None] * EC + jnp.arange(EC * H)[None, None, :] // H)
    ex = (lane == N_GROUPS + col_expert).astype(BF16)
    pad = ROUTER_LANES - N_GROUPS - N_EXPERTS
    wr = jnp.concatenate([router_g_w, router_e_w, jnp.zeros((L, D, pad), F32)], axis=-1)
    br = jnp.concatenate([router_g_b, router_e_b, jnp.zeros((L, pad), F32)], axis=-1).reshape(L, 1, ROUTER_LANES)
    bias = _na_bias_tables(na_rpb)
    cos, sin, perm = _rope_tables(n_lat)

    mod = _adaln(cc, ada_w_down, ada_w_up, ada_b)
    xs = jnp.concatenate([x[0], ctx[0]], axis=0)
    u = _modulate(xs, mod, 0, 0, 1, n_lat, T)

    for l in range(L):
        last = l == L - 1
        p = _matmul(u, w_in_b, l, tm_mm, tn_mm(w_in.shape[-1]))
        hd = _local_mix(p, conv_dw, conv_dw_b, conv_ln_g, conv_ln_b, l, n_lat=n_lat, n_ctx=n_ctx, T=T, CW=CW)
        qk = _rope(p, cos, sin, perm, n_lat=n_lat, T=T, CW=CW, q_blk=3)
        na = _na(p, qk, bias, l, n_lat=n_lat, n_ctx=n_ctx, CW=CW, HG=HG)
        merged = _merge(hd, na, p, w_conv_out_b, pool_w_b, w_na_out_b, pool_scale, l, CW=CW, tm=tm_mid)
        y = _matmul(merged, w_out_b, l, tm_mm, tn_mm(D))
        xs, uext = _residual_ln(xs, y, mod, l, 2, ln1_g, ln1_b, n_lat=n_lat, tm=T, alpha=alpha, rows=M,
                                nxt=(l, 3, 4), router=(wr, br))
        f = _moe(uext, wgu, wd, ex, l, tA=tA)
        if last:
            (xs,) = _residual_ln(xs, f, mod, l, 5, ln2_g, ln2_b, n_lat=n_lat, tm=T, alpha=alpha, rows=n_lat)
        else:
            xs, u = _residual_ln(xs, f, mod, l, 5, ln2_g, ln2_b, n_lat=n_lat, tm=T, alpha=alpha, rows=M,
                                 nxt=(l + 1, 0, 1))
    return xs[None]
```

```python
import functools

import jax
import jax.numpy as jnp
from jax import lax
from jax.experimental import pallas as pl
from jax.experimental.pallas import tpu as pltpu

F32 = jnp.float32
BF16 = jnp.bfloat16
F8 = jnp.float8_e4m3fn
HIGHEST = lax.Precision.HIGHEST
F8_TARGET = 240.0
F8_TINY = 1e-30

GRID_W = 64
CONV_K = 31
POOL_WINDOWS = (2, 4, 8, 16)
HEAD_DIM = 128
NA_KH = 8
NA_KW = 16
N_GROUPS = 4
EXP_PER_GROUP = 8
N_EXPERTS = N_GROUPS * EXP_PER_GROUP
N_MOD = 6
ROPE_THETA = 10000.0
LN_EPS = 1e-5
NEG = -1e30

VMEM_LIMIT_BYTES = 56 * 1024 * 1024
LANES = 128
HALO = 16
ROUTER_LANES = 128
EXPERTS_PER_CHUNK = 2
NA_QROWS = NA_KH // 2
MOE_SUB_ROWS = 256


def _params(sem):
    return pltpu.CompilerParams(dimension_semantics=sem, vmem_limit_bytes=VMEM_LIMIT_BYTES)


def _pick(n, cands):
    for c in cands:
        if n % c == 0:
            return c
    raise ValueError(f"no tile in {cands} divides {n}")


def _sigmoid(x):
    return 1.0 / (1.0 + jnp.exp(-x))


def _adaln_kernel(c_ref, wd_ref, wu_ref, b_ref, o_ref):
    c = c_ref[...]
    s = c * _sigmoid(c)
    t = jnp.dot(s, wd_ref[...], precision=HIGHEST, preferred_element_type=F32)
    o_ref[...] = jnp.dot(t, wu_ref[...], precision=HIGHEST, preferred_element_type=F32) + b_ref[...]


def _adaln(cc, w_down, w_up, b):
    L, D, R = w_down.shape
    return pl.pallas_call(
        _adaln_kernel,
        out_shape=jax.ShapeDtypeStruct((L, 8, N_MOD * D), F32),
        grid=(L, N_MOD),
        in_specs=[
            pl.BlockSpec((8, D), lambda l, j: (0, 0)),
            pl.BlockSpec((None, D, R), lambda l, j: (l, 0, 0)),
            pl.BlockSpec((None, R, D), lambda l, j: (l, 0, j)),
            pl.BlockSpec((None, 1, D), lambda l, j: (l, 0, j)),
        ],
        out_specs=pl.BlockSpec((None, 8, D), lambda l, j: (l, 0, j)),
        compiler_params=_params(("parallel", "parallel")),
        name="adaln",
    )(cc, w_down, w_up, b.reshape(L, 1, N_MOD * D))


def _pick_row(ref, is_ctx):
    return jnp.where(is_ctx, ref[1:2, :], ref[0:1, :])


def _store_u(u, u_ref, u8_ref, rs_ref):
    u_ref[...] = u.astype(BF16)
    amax = jnp.max(jnp.abs(u), axis=-1, keepdims=True)
    scale = F8_TARGET / jnp.maximum(amax, F8_TINY)
    u8_ref[...] = (u * scale).astype(F8)
    rs_ref[...] = jnp.broadcast_to(1.0 / scale, rs_ref.shape)


def _u_outputs(rows, D, tm):
    shapes = [jax.ShapeDtypeStruct((rows, D), BF16), jax.ShapeDtypeStruct((rows, D), F8),
              jax.ShapeDtypeStruct((rows, LANES), F32)]
    specs = [pl.BlockSpec((tm, D), lambda i: (i, 0)), pl.BlockSpec((tm, D), lambda i: (i, 0)),
             pl.BlockSpec((tm, LANES), lambda i: (i, 0))]
    return shapes, specs


def _modulate_kernel(x_ref, sh_ref, sc_ref, u_ref, u8_ref, rs_ref, *, n_lat_tiles):
    is_ctx = pl.program_id(0) >= n_lat_tiles
    u = x_ref[...] * (1.0 + _pick_row(sc_ref, is_ctx)) + _pick_row(sh_ref, is_ctx)
    _store_u(u, u_ref, u8_ref, rs_ref)


def _modulate(x, mod, l, k_sh, k_sc, n_lat, tm):
    M, D = x.shape
    shapes, specs = _u_outputs(M, D, tm)
    return pl.pallas_call(
        functools.partial(_modulate_kernel, n_lat_tiles=n_lat // tm),
        out_shape=shapes,
        grid=(M // tm,),
        in_specs=[
            pl.BlockSpec((tm, D), lambda i: (i, 0)),
            pl.BlockSpec((None, 8, D), lambda i: (l, 0, k_sh)),
            pl.BlockSpec((None, 8, D), lambda i: (l, 0, k_sc)),
        ],
        out_specs=specs,
        compiler_params=_params(("parallel",)),
        name="modulate",
    )(x, mod, mod)


def _route(logits):
    lane = lax.broadcasted_iota(jnp.int32, logits.shape, 1).astype(F32)
    big = float(ROUTER_LANES)
    gl = jnp.where(lane < N_GROUPS, logits, NEG)
    gmax = jnp.max(gl, axis=-1, keepdims=True)
    gidx = jnp.min(jnp.where(gl == gmax, lane, big), axis=-1, keepdims=True)
    p_group = 1.0 / jnp.sum(jnp.exp(gl - gmax), axis=-1, keepdims=True)
    e = lane - N_GROUPS
    in_group = (e >= gidx * EXP_PER_GROUP) & (e < (gidx + 1) * EXP_PER_GROUP)
    el = jnp.where(in_group, logits, NEG)
    t1 = jnp.max(el, axis=-1, keepdims=True)
    i1 = jnp.min(jnp.where(el == t1, lane, big), axis=-1, keepdims=True)
    el2 = jnp.where(lane == i1, NEG, el)
    t2 = jnp.max(el2, axis=-1, keepdims=True)
    i2 = jnp.min(jnp.where(el2 == t2, lane, big), axis=-1, keepdims=True)
    e2 = jnp.exp(t2 - t1)
    w1 = p_group / (1.0 + e2)
    w2 = p_group * e2 / (1.0 + e2)
    cw = jnp.where(lane == i1, w1, 0.0) + jnp.where(lane == i2, w2, 0.0)
    return jnp.where(lane == 0.0, gidx, cw)


def _ln_kernel(*refs, n_lat_tiles, alpha, with_u, with_router):
    x_ref, y_ref, gate_ref, lng_ref, lnb_ref = refs[:5]
    pos = 5
    if with_u:
        sh_ref, sc_ref = refs[pos:pos + 2]
        pos += 2
    if with_router:
        wrh_ref, wrl_ref, br_ref = refs[pos:pos + 3]
        pos += 3
    xo_ref = refs[pos]
    pos += 1
    is_ctx = pl.program_id(0) >= n_lat_tiles
    z = alpha * x_ref[...] + _pick_row(gate_ref, is_ctx) * y_ref[...].astype(F32)
    mu = jnp.mean(z, axis=-1, keepdims=True)
    zc = z - mu
    var = jnp.mean(zc * zc, axis=-1, keepdims=True)
    xn = zc * lax.rsqrt(var + LN_EPS) * lng_ref[...] + lnb_ref[...]
    xo_ref[...] = xn
    if with_u:
        u = xn * (1.0 + _pick_row(sc_ref, is_ctx)) + _pick_row(sh_ref, is_ctx)
        if with_router:
            u_ref = refs[pos]
            D = u.shape[-1]
            u_hi = u.astype(BF16)
            u_lo = (u - u_hi.astype(F32)).astype(BF16)
            wr_hi = wrh_ref[...]
            logits = (jnp.dot(u_hi, wr_hi, preferred_element_type=F32)
                      + jnp.dot(u_lo, wr_hi, preferred_element_type=F32)
                      + jnp.dot(u_hi, wrl_ref[...], preferred_element_type=F32) + br_ref[...])
            u_ref[:, :D] = u
            u_ref[:, D:] = _route(logits)
        else:
            _store_u(u, *refs[pos:pos + 3])


def _residual_ln(x, y, mod, l, k_gate, ln_g, ln_b, *, n_lat, tm, alpha, rows, nxt=None, router=None):
    M, D = x.shape
    L = ln_g.shape[0]
    in_specs = [
        pl.BlockSpec((tm, D), lambda i: (i, 0)),
        pl.BlockSpec((tm, D), lambda i: (i, 0)),
        pl.BlockSpec((None, 8, D), lambda i: (l, 0, k_gate)),
        pl.BlockSpec((None, 1, D), lambda i: (l, 0, 0)),
        pl.BlockSpec((None, 1, D), lambda i: (l, 0, 0)),
    ]
    args = [x, y, mod, ln_g.reshape(L, 1, D), ln_b.reshape(L, 1, D)]
    out_shape = [jax.ShapeDtypeStruct((rows, D), F32)]
    out_specs = [pl.BlockSpec((tm, D), lambda i: (i, 0))]
    if nxt is not None:
        ln, k_sh, k_sc = nxt
        in_specs += [pl.BlockSpec((None, 8, D), lambda i: (ln, 0, k_sh)),
                     pl.BlockSpec((None, 8, D), lambda i: (ln, 0, k_sc))]
        args += [mod, mod]
        if router is None:
            shapes, specs = _u_outputs(rows, D, tm)
            out_shape += shapes
            out_specs += specs
        else:
            out_shape.append(jax.ShapeDtypeStruct((rows, D + ROUTER_LANES), F32))
            out_specs.append(pl.BlockSpec((tm, D + ROUTER_LANES), lambda i: (i, 0)))
    if router is not None:
        wr_hi, wr_lo, br = router
        in_specs += [pl.BlockSpec((None, D, ROUTER_LANES), lambda i: (l, 0, 0)),
                     pl.BlockSpec((None, D, ROUTER_LANES), lambda i: (l, 0, 0)),
                     pl.BlockSpec((None, 1, ROUTER_LANES), lambda i: (l, 0, 0))]
        args += [wr_hi, wr_lo, br]
    return pl.pallas_call(
        functools.partial(_ln_kernel, n_lat_tiles=n_lat // tm, alpha=alpha,
                          with_u=nxt is not None, with_router=router is not None),
        out_shape=out_shape,
        grid=(rows // tm,),
        in_specs=in_specs,
        out_specs=out_specs,
        compiler_params=_params(("parallel",)),
        name="residual_ln",
    )(*args)


def _mm_kernel(a_ref, b_ref, o_ref):
    o_ref[...] = jnp.dot(a_ref[...], b_ref[...], preferred_element_type=F32).astype(o_ref.dtype)


def _matmul(a, b, l, tm, tn):
    M, K = a.shape
    N = b.shape[-1]
    return pl.pallas_call(
        _mm_kernel,
        out_shape=jax.ShapeDtypeStruct((M, N), BF16),
        grid=(M // tm, N // tn),
        in_specs=[pl.BlockSpec((tm, K), lambda i, j: (i, 0)),
                  pl.BlockSpec((None, K, tn), lambda i, j: (l, 0, j))],
        out_specs=pl.BlockSpec((tm, tn), lambda i, j: (i, j)),
        compiler_params=_params(("parallel", "arbitrary")),
        name="matmul",
    )(a, b)


def _mm_f8_kernel(a_ref, rs_ref, b_ref, cs_ref, o_ref):
    acc = jnp.dot(a_ref[...], b_ref[...], preferred_element_type=F32)
    o_ref[...] = (acc * rs_ref[:, 0:1] * cs_ref[...]).astype(o_ref.dtype)


def _matmul_f8(a8, rs, b8, cs, l, tm, tn):
    M, K = a8.shape
    N = b8.shape[-1]
    return pl.pallas_call(
        _mm_f8_kernel,
        out_shape=jax.ShapeDtypeStruct((M, N), BF16),
        grid=(M // tm, N // tn),
        in_specs=[pl.BlockSpec((tm, K), lambda i, j: (i, 0)),
                  pl.BlockSpec((tm, LANES), lambda i, j: (i, 0)),
                  pl.BlockSpec((None, K, tn), lambda i, j: (l, 0, j)),
                  pl.BlockSpec((None, 1, tn), lambda i, j: (l, 0, j))],
        out_specs=pl.BlockSpec((tm, tn), lambda i, j: (i, j)),
        compiler_params=_params(("parallel", "arbitrary")),
        name="matmul_f8",
    )(a8, rs, b8, cs)


def _local_kernel(prev_ref, cur_ref, next_ref, dw_ref, dwb_ref, lng_ref, lnb_ref, o_ref,
                  glu_s, z_s, y_s, *, T, CW, GW, n_lat, n_ctx):
    s = pl.program_id(0) * T
    in_ctx = s >= n_lat
    seq_start = jnp.where(in_ctx, n_lat, 0)
    seq_len = jnp.where(in_ctx, n_ctx, n_lat)
    keep_prev = jnp.where(s == seq_start, 0.0, 1.0)
    keep_next = jnp.where(s + T == seq_start + seq_len, 0.0, 1.0)

    def stage(ref, row0, rows, keep):
        blk = ref[...]
        a = blk[:, :CW].astype(F32)
        g = blk[:, CW:2 * CW].astype(F32)
        z = blk[:, 2 * CW:3 * CW].astype(F32)
        glu = a * _sigmoid(g)
        if keep is not None:
            glu = glu * keep
            z = z * keep
        glu_s[row0:row0 + rows, :] = glu
        z_s[row0:row0 + rows, :] = z

    stage(prev_ref, 0, HALO, keep_prev)
    stage(cur_ref, HALO, T, None)
    stage(next_ref, HALO + T, HALO, keep_next)

    RC = min(T, 128)
    base = HALO - CONV_K // 2
    for rc in range(T // RC):
        for lc in range(CW // LANES):
            cols = slice(lc * LANES, (lc + 1) * LANES)
            acc = jnp.zeros((RC, LANES), F32)
            for j in range(CONV_K):
                r0 = rc * RC + base + j
                acc = acc + glu_s[r0:r0 + RC, cols] * dw_ref[j:j + 1, cols]
            y_s[rc * RC:(rc + 1) * RC, cols] = acc + dwb_ref[:, cols]

    y = y_s[...]
    mu = jnp.mean(y, axis=-1, keepdims=True)
    yc = y - mu
    var = jnp.mean(yc * yc, axis=-1, keepdims=True)
    hn = yc * lax.rsqrt(var + LN_EPS) * lng_ref[...] + lnb_ref[...]
    o_ref[:, :CW] = (hn * _sigmoid(hn)).astype(BF16)

    pos = lax.broadcasted_iota(jnp.int32, (T, 1), 0) + (s - seq_start)
    for gi, w in enumerate(POOL_WINDOWS):
        cols = slice(gi * GW, (gi + 1) * GW)
        tot = jnp.zeros((T, GW), F32)
        for o in range(-(w // 2), w - w // 2):
            tot = tot + z_s[HALO + o:HALO + o + T, cols]
        cnt = jnp.minimum(pos + (w - w // 2), seq_len) - jnp.maximum(pos - w // 2, 0)
        d = tot / cnt.astype(F32) - z_s[HALO:HALO + T, cols]
        o_ref[:, CW + gi * GW:CW + (gi + 1) * GW] = d.astype(BF16)


def _local_mix(p, conv_dw, conv_dw_b, conv_ln_g, conv_ln_b, l, *, n_lat, n_ctx, T, CW):
    M = p.shape[0]
    L = conv_dw.shape[0]
    GW = CW // len(POOL_WINDOWS)
    hb = T // HALO
    n_hblk = M // HALO
    return pl.pallas_call(
        functools.partial(_local_kernel, T=T, CW=CW, GW=GW, n_lat=n_lat, n_ctx=n_ctx),
        out_shape=jax.ShapeDtypeStruct((M, 2 * CW), BF16),
        grid=(M // T,),
        in_specs=[
            pl.BlockSpec((HALO, 3 * CW), lambda i: (jnp.maximum(i * hb - 1, 0), 0)),
            pl.BlockSpec((T, 3 * CW), lambda i: (i, 0)),
            pl.BlockSpec((HALO, 3 * CW), lambda i: (jnp.minimum((i + 1) * hb, n_hblk - 1), 0)),
            pl.BlockSpec((None, CONV_K, CW), lambda i: (l, 0, 0)),
            pl.BlockSpec((None, 1, CW), lambda i: (l, 0, 0)),
            pl.BlockSpec((None, 1, CW), lambda i: (l, 0, 0)),
            pl.BlockSpec((None, 1, CW), lambda i: (l, 0, 0)),
        ],
        out_specs=pl.BlockSpec((T, 2 * CW), lambda i: (i, 0)),
        scratch_shapes=[pltpu.VMEM((T + 2 * HALO, CW), F32),
                        pltpu.VMEM((T + 2 * HALO, CW), F32),
                        pltpu.VMEM((T, CW), F32)],
        compiler_params=_params(("parallel",)),
        name="local_mix",
    )(p, p, p, conv_dw, conv_dw_b.reshape(L, 1, CW), conv_ln_g.reshape(L, 1, CW),
      conv_ln_b.reshape(L, 1, CW))


def _rope_kernel(x_ref, cos_ref, sin_ref, perm_ref, o_ref, *, CW):
    cos = cos_ref[...]
    sin = sin_ref[...]
    perm = perm_ref[...]
    for h in range(CW // HEAD_DIM):
        cols = slice(h * HEAD_DIM, (h + 1) * HEAD_DIM)
        xb = x_ref[:, cols]
        partner = jnp.dot(xb, perm, preferred_element_type=F32)
        o_ref[:, cols] = (xb.astype(F32) * cos + partner * sin).astype(BF16)


def _rope_tables(n_lat):
    t = jnp.arange(n_lat)
    r = (t // GRID_W).astype(F32)
    col = (t % GRID_W).astype(F32)
    axis_dim = HEAD_DIM // 2
    inv = ROPE_THETA ** (-jnp.arange(0, axis_dim, 2, dtype=F32) / axis_dim)
    ang_r = r[:, None] * inv[None, :]
    ang_c = col[:, None] * inv[None, :]
    cos = jnp.concatenate([jnp.cos(ang_r), jnp.cos(ang_r), jnp.cos(ang_c), jnp.cos(ang_c)], axis=-1)
    sin = jnp.concatenate([-jnp.sin(ang_r), jnp.sin(ang_r), -jnp.sin(ang_c), jnp.sin(ang_c)], axis=-1)
    q = axis_dim // 2
    i = jnp.arange(HEAD_DIM)
    src = jnp.where((i % axis_dim) < q, i + q, i - q)
    perm = (jnp.arange(HEAD_DIM)[:, None] == src[None, :]).astype(BF16)
    return cos, sin, perm


def _rope(p, cos, sin, perm, *, n_lat, T, CW, q_blk):
    return pl.pallas_call(
        functools.partial(_rope_kernel, CW=CW),
        out_shape=jax.ShapeDtypeStruct((n_lat, 4 * CW), BF16),
        grid=(n_lat // T, 4),
        in_specs=[
            pl.BlockSpec((T, CW), lambda i, j: (i, q_blk + j)),
            pl.BlockSpec((T, HEAD_DIM), lambda i, j: (i, 0)),
            pl.BlockSpec((T, HEAD_DIM), lambda i, j: (i, 0)),
            pl.BlockSpec((HEAD_DIM, HEAD_DIM), lambda i, j: (0, 0)),
        ],
        out_specs=pl.BlockSpec((T, CW), lambda i, j: (i, j)),
        compiler_params=_params(("parallel", "arbitrary")),
        name="rope",
    )(p, cos, sin, perm)


def _na_bias_tables(rpb):
    W = GRID_W
    w = jnp.arange(W)
    c0 = jnp.clip(w - NA_KW // 2, 0, W - NA_KW)
    kc = jnp.arange(W)
    dc = kc[None, :] - w[:, None] + (NA_KW - 1)
    valid = (kc[None, :] >= c0[:, None]) & (kc[None, :] < c0[:, None] + NA_KW)
    onehot = (jnp.arange(2 * NA_KW - 1)[:, None, None] == dc[None, :, :]) & valid[None]
    bt = jnp.einsum('lhrd,dwk->lhrwk', rpb, onehot.astype(F32), precision=HIGHEST)
    bt = jnp.where(valid[None, None, None], bt, NEG)
    QR = NA_QROWS
    L, H = rpb.shape[:2]
    t = jnp.stack([bt[:, :, NA_KH - 1 - QR - a:NA_KH - 1 - QR - a + 3 * QR] for a in range(QR)], axis=2)
    a = jnp.arange(QR)[:, None]
    b = jnp.arange(3 * QR)[None, :]
    valid_rows = jnp.stack([(b - a >= 0) & (b - a < NA_KH),
                            jnp.broadcast_to((b >= QR) & (b < QR + NA_KH), (QR, 3 * QR)),
                            jnp.broadcast_to(b < NA_KH, (QR, 3 * QR))])
    t = jnp.where(valid_rows[None, :, None, :, :, None, None], t[:, None], NEG)
    t = jnp.transpose(t, (0, 1, 2, 3, 5, 4, 6))
    return t.reshape(L, 3, H, QR * W, 3 * QR * W)


def _dot_nt(a, b):
    return lax.dot_general(a, b, (((1,), (1,)), ((), ())), preferred_element_type=F32)


def _na_kernel(qr_ref, qp_ref, kp_ref, kcur_ref, kn_ref, vp_ref, vcur_ref, vn_ref, kc_ref, vc_ref,
               bias_ref, o_ref, *, HG, nblk, scale):
    rb = pl.program_id(1)
    B = NA_QROWS * GRID_W

    @pl.when(rb < nblk)
    def _():
        for h in range(HG):
            cols = slice(h * HEAD_DIM, (h + 1) * HEAD_DIM)
            q = qr_ref[:, cols]
            s = jnp.concatenate([_dot_nt(q, kp_ref[:, cols]), _dot_nt(q, kcur_ref[:, cols]),
                                 _dot_nt(q, kn_ref[:, cols])], axis=1) * scale + bias_ref[h]
            sc = _dot_nt(qp_ref[:, cols], kc_ref[:, cols]) * scale
            m = jnp.maximum(jnp.max(s, axis=-1, keepdims=True), jnp.max(sc, axis=-1, keepdims=True))
            e = jnp.exp(s - m)
            ec = jnp.exp(sc - m)
            den = jnp.sum(e, axis=-1, keepdims=True) + jnp.sum(ec, axis=-1, keepdims=True)
            eb = e.astype(BF16)
            o = (jnp.dot(eb[:, 0:B], vp_ref[:, cols], preferred_element_type=F32)
                 + jnp.dot(eb[:, B:2 * B], vcur_ref[:, cols], preferred_element_type=F32)
                 + jnp.dot(eb[:, 2 * B:3 * B], vn_ref[:, cols], preferred_element_type=F32)
                 + jnp.dot(ec.astype(BF16), vc_ref[:, cols], preferred_element_type=F32))
            o_ref[:, cols] = (o / den).astype(BF16)

    @pl.when(rb == nblk)
    def _():
        for h in range(HG):
            cols = slice(h * HEAD_DIM, (h + 1) * HEAD_DIM)
            s = _dot_nt(qp_ref[:, cols], kc_ref[:, cols]) * scale
            m = jnp.max(s, axis=-1, keepdims=True)
            e = jnp.exp(s - m)
            o = jnp.dot(e.astype(BF16), vc_ref[:, cols], preferred_element_type=F32)
            o_ref[:, cols] = (o / jnp.sum(e, axis=-1, keepdims=True)).astype(BF16)


def _na(p, qk, bias, l, *, n_lat, n_ctx, CW, HG):
    M = p.shape[0]
    NA_W = 2 * CW
    heads = NA_W // HEAD_DIM
    blk = NA_QROWS * GRID_W
    nblk = n_lat // blk
    assert n_ctx == blk and nblk >= 2
    bw = HG * HEAD_DIM
    q_c, k_c, v_c = 3 * CW // bw, 5 * CW // bw, 7 * CW // bw
    kr_c = NA_W // bw
    ctx_blk = n_lat // n_ctx
    cur = lambda r: jnp.minimum(r, nblk - 1)
    prev = lambda r: jnp.maximum(cur(r) - 1, 0)
    nxt = lambda r: jnp.minimum(r + 1, nblk - 1)
    cls = lambda r: jnp.where(r == 0, 1, jnp.where(r >= nblk - 1, 2, 0))
    return pl.pallas_call(
        functools.partial(_na_kernel, HG=HG, nblk=nblk, scale=HEAD_DIM ** -0.5),
        out_shape=jax.ShapeDtypeStruct((M, NA_W), BF16),
        grid=(heads // HG, nblk + 1),
        in_specs=[
            pl.BlockSpec((blk, bw), lambda g, r: (cur(r), g)),
            pl.BlockSpec((blk, bw), lambda g, r: (r, q_c + g)),
            pl.BlockSpec((blk, bw), lambda g, r: (prev(r), kr_c + g)),
            pl.BlockSpec((blk, bw), lambda g, r: (cur(r), kr_c + g)),
            pl.BlockSpec((blk, bw), lambda g, r: (nxt(r), kr_c + g)),
            pl.BlockSpec((blk, bw), lambda g, r: (prev(r), v_c + g)),
            pl.BlockSpec((blk, bw), lambda g, r: (cur(r), v_c + g)),
            pl.BlockSpec((blk, bw), lambda g, r: (nxt(r), v_c + g)),
            pl.BlockSpec((n_ctx, bw), lambda g, r: (ctx_blk, k_c + g)),
            pl.BlockSpec((n_ctx, bw), lambda g, r: (ctx_blk, v_c + g)),
            pl.BlockSpec((None, None, HG, blk, 3 * blk), lambda g, r: (l, cls(r), g, 0, 0)),
        ],
        out_specs=pl.BlockSpec((blk, bw), lambda g, r: (r, g)),
        compiler_params=_params(("parallel", "arbitrary")),
        name="neighbourhood_attention",
    )(qk, p, qk, qk, qk, p, p, p, p, p, bias)


def _merge_kernel(h_ref, d_ref, na_ref, g0_ref, g1_ref, g2_ref, wc_ref, wp_ref, wn_ref, ps_ref, o_ref):
    yc = jnp.dot(h_ref[...], wc_ref[...], preferred_element_type=F32)
    yp = jnp.dot(d_ref[...], wp_ref[...], preferred_element_type=F32) * ps_ref[...]
    yn = jnp.dot(na_ref[...], wn_ref[...], preferred_element_type=F32)
    merged = (_sigmoid(g0_ref[...].astype(F32)) * yc + _sigmoid(g1_ref[...].astype(F32)) * yp
              + _sigmoid(g2_ref[...].astype(F32)) * yn)
    o_ref[...] = merged.astype(BF16)


def _merge(hd, na, pg, w_conv_out, pool_w, w_na_out, pool_scale, l, *, CW, tm):
    M = hd.shape[0]
    D = 4 * CW
    L = pool_scale.shape[0]
    G = len(POOL_WINDOWS)
    GW = CW // G
    gate_blk = 0
    return pl.pallas_call(
        _merge_kernel,
        out_shape=jax.ShapeDtypeStruct((M, D), BF16),
        grid=(M // tm, G),
        in_specs=[
            pl.BlockSpec((tm, CW), lambda i, j: (i, 0)),
            pl.BlockSpec((tm, GW), lambda i, j: (i, G + j)),
            pl.BlockSpec((tm, 2 * CW), lambda i, j: (i, 0)),
            pl.BlockSpec((tm, CW), lambda i, j: (i, gate_blk + j)),
            pl.BlockSpec((tm, CW), lambda i, j: (i, gate_blk + G + j)),
            pl.BlockSpec((tm, CW), lambda i, j: (i, gate_blk + 2 * G + j)),
            pl.BlockSpec((None, CW, CW), lambda i, j: (l, 0, j)),
            pl.BlockSpec((None, None, GW, CW), lambda i, j: (l, j, 0, 0)),
            pl.BlockSpec((None, 2 * CW, CW), lambda i, j: (l, 0, j)),
            pl.BlockSpec((None, 1, CW), lambda i, j: (l, 0, j)),
        ],
        out_specs=pl.BlockSpec((tm, CW), lambda i, j: (i, j)),
        compiler_params=_params(("parallel", "arbitrary")),
        name="merge_branches",
    )(hd, hd, na, pg, pg, pg, w_conv_out, pool_w, w_na_out, pool_scale.reshape(L, 1, D))


def _moe_kernel(tg_ref, src_ref, dst_ref, u_hbm, wgu_ref, wd_ref, ex_ref, f_hbm,
                gbuf, ubf, cwb, acc, obuf, sem_in, sem_out, *, tA, D, EH, n_tiles, NC):
    del tg_ref
    t = pl.program_id(0)
    c = pl.program_id(1)

    def start_gather(tile):
        def body(r, carry):
            tok = src_ref[tile * tA + r]
            pltpu.make_async_copy(u_hbm.at[pl.ds(tok, 1)], gbuf.at[pl.ds(r, 1)], sem_in).start()
            return carry
        lax.fori_loop(0, tA, body, 0, unroll=8)

    def start_scatter(tile):
        def body(r, carry):
            tok = dst_ref[tile * tA + r]
            pltpu.make_async_copy(obuf.at[pl.ds(r, 1)], f_hbm.at[pl.ds(tok, 1)], sem_out).start()
            return carry
        lax.fori_loop(0, tA, body, 0, unroll=8)

    def wait_rows(buf, sem):
        pltpu.make_async_copy(buf, buf, sem).wait()

    @pl.when((t == 0) & (c == 0))
    def _():
        start_gather(0)
        acc[...] = jnp.zeros_like(acc)

    @pl.when(c == 0)
    def _():
        wait_rows(gbuf, sem_in)
        ubf[...] = gbuf[:, :D].astype(BF16)
        cwb[...] = gbuf[:, D:].astype(BF16)

        @pl.when(t + 1 < n_tiles)
        def _():
            start_gather(t + 1)

    for sb in range(tA // MOE_SUB_ROWS):
        rows = slice(sb * MOE_SUB_ROWS, (sb + 1) * MOE_SUB_ROWS)
        gu = jnp.dot(ubf[rows, :], wgu_ref[...], preferred_element_type=F32)
        g = gu[:, :EH]
        up = gu[:, EH:]
        cwx = jnp.dot(cwb[rows, :], ex_ref[...], preferred_element_type=F32)
        h = (g * _sigmoid(g)) * up * cwx
        contrib = jnp.dot(h.astype(BF16), wd_ref[...], preferred_element_type=F32)
        acc[rows, :] = jnp.where(c == 0, 0.0, acc[rows, :]) + contrib

    @pl.when(c == NC - 1)
    def _():
        @pl.when(t > 0)
        def _():
            wait_rows(obuf, sem_out)

        obuf[...] = acc[...]
        start_scatter(t)

        @pl.when(t == n_tiles - 1)
        def _():
            wait_rows(obuf, sem_out)


def _moe_plan(group, M, tA, n_tiles):
    onehot = (group[:, None] == jnp.arange(N_GROUPS)[None, :]).astype(jnp.int32)
    rank = jnp.sum((jnp.cumsum(onehot, axis=0) - onehot) * onehot, axis=1)
    tiles_g = (jnp.sum(onehot, axis=0) + tA - 1) // tA
    first_tile = jnp.cumsum(tiles_g) - tiles_g
    pos = first_tile[group] * tA + rank
    tok = jnp.arange(M, dtype=jnp.int32)
    slot_tok = jnp.full((n_tiles * tA,), -1, jnp.int32).at[pos].set(tok)
    src = jnp.maximum(slot_tok, 0)
    dst = jnp.where(slot_tok < 0, M + jnp.arange(n_tiles * tA, dtype=jnp.int32) % tA, slot_tok)
    tile_group = jnp.sum(jnp.arange(n_tiles)[:, None] >= first_tile[None, :], axis=1) - 1
    return tile_group.astype(jnp.int32), src, dst


def _moe(uext, wgu, wd, ex, l, *, tA):
    M = uext.shape[0]
    D = uext.shape[1] - ROUTER_LANES
    EH = wd.shape[2]
    NC = EXP_PER_GROUP // EXPERTS_PER_CHUNK
    n_tiles = (M + N_GROUPS * (tA - 1)) // tA
    tile_group, src, dst = _moe_plan(uext[:, D].astype(jnp.int32), M, tA, n_tiles)
    grid_spec = pltpu.PrefetchScalarGridSpec(
        num_scalar_prefetch=3,
        grid=(n_tiles, NC),
        in_specs=[
            pl.BlockSpec(memory_space=pl.ANY),
            pl.BlockSpec((None, None, D, 2 * EH), lambda t, c, tg, s, d: (l, tg[t] * NC + c, 0, 0)),
            pl.BlockSpec((None, None, EH, D), lambda t, c, tg, s, d: (l, tg[t] * NC + c, 0, 0)),
            pl.BlockSpec((None, ROUTER_LANES, EH), lambda t, c, tg, s, d: (tg[t] * NC + c, 0, 0)),
        ],
        out_specs=pl.BlockSpec(memory_space=pl.ANY),
        scratch_shapes=[
            pltpu.VMEM((tA, D + ROUTER_LANES), F32),
            pltpu.VMEM((tA, D), BF16),
            pltpu.VMEM((tA, ROUTER_LANES), BF16),
            pltpu.VMEM((tA, D), F32),
            pltpu.VMEM((tA, D), F32),
            pltpu.SemaphoreType.DMA(()),
            pltpu.SemaphoreType.DMA(()),
        ],
    )
    return pl.pallas_call(
        functools.partial(_moe_kernel, tA=tA, D=D, EH=EH, n_tiles=n_tiles, NC=NC),
        out_shape=jax.ShapeDtypeStruct((M + tA, D), F32),
        grid_spec=grid_spec,
        compiler_params=_params(("arbitrary", "arbitrary")),
        name="moe_experts",
    )(tile_group, src, dst, uext, wgu, wd, ex)


def kernel(x, c, ctx, c_ctx, ada_w_down, ada_w_up, ada_b, w_in, conv_dw, conv_dw_b, conv_ln_g,
           conv_ln_b, w_conv_out, pool_w, pool_scale, na_rpb, w_na_out, w_out, ln1_g, ln1_b,
           router_g_w, router_g_b, router_e_w, router_e_b, moe_w_gate, moe_w_up, moe_w_down,
           ln2_g, ln2_b):
    B, n_lat, D = x.shape
    n_ctx = ctx.shape[1]
    L = w_in.shape[0]
    assert B == 1 and D % 16 == 0
    CW = D // 4
    M = n_lat + n_ctx
    rows = n_lat // GRID_W
    H = moe_w_gate.shape[-1]
    alpha = (2 * L) ** 0.25
    assert n_lat % (NA_QROWS * GRID_W) == 0 and rows >= NA_KH and n_lat % n_ctx == 0
    assert (CW // len(POOL_WINDOWS)) % 8 == 0 and CW % HEAD_DIM == 0

    T = _pick(n_ctx, (256, 128, 64))
    tm_mm = _pick(M, (1280, 640, 320, 256, 128))
    tn_mm = lambda n: _pick(n, (512, 384, 256, 128))
    tm_mid = _pick(M, (640, 320, 256, 128))
    tA = min(512, T * 2)
    HG = _pick(CW // HEAD_DIM, (4, 2, 1))

    cc = jnp.zeros((8, D), F32).at[0].set(c[0]).at[1].set(c_ctx)
    n_main = 9 * CW
    w_in_b = w_in[:, :, :n_main].astype(BF16)
    w_gate = w_in[:, :, n_main:]
    col_scale = F8_TARGET / jnp.maximum(jnp.max(jnp.abs(w_gate), axis=1, keepdims=True), F8_TINY)
    w_gate8 = (w_gate * col_scale).astype(F8)
    w_gate_rs = 1.0 / col_scale
    w_conv_out_b = w_conv_out.astype(BF16)
    pool_w_b = pool_w.astype(BF16)
    w_na_out_b = w_na_out.astype(BF16)
    w_out_b = w_out.astype(BF16)
    EC = EXPERTS_PER_CHUNK
    NCH = N_EXPERTS // EC

    def chunked(w):
        return jnp.transpose(w.astype(BF16).reshape(L, NCH, EC, D, H), (0, 1, 3, 2, 4)).reshape(L, NCH, D, EC * H)

    wgu = jnp.concatenate([chunked(moe_w_gate), chunked(moe_w_up)], axis=-1)
    wd = moe_w_down.astype(BF16).reshape(L, NCH, EC * H, D)
    lane = jnp.arange(ROUTER_LANES)[None, :, None]
    col_expert = (jnp.arange(NCH)[:, None, None] * EC + jnp.arange(EC * H)[None, None, :] // H)
    ex = (lane == N_GROUPS + col_expert).astype(BF16)
    pad = ROUTER_LANES - N_GROUPS - N_EXPERTS
    wr = jnp.concatenate([router_g_w, router_e_w, jnp.zeros((L, D, pad), F32)], axis=-1)
    wr_hi = wr.astype(BF16)
    wr_lo = (wr - wr_hi.astype(F32)).astype(BF16)
    br = jnp.concatenate([router_g_b, router_e_b, jnp.zeros((L, pad), F32)], axis=-1).reshape(L, 1, ROUTER_LANES)
    bias = _na_bias_tables(na_rpb)
    cos, sin, perm = _rope_tables(n_lat)

    mod = _adaln(cc, ada_w_down, ada_w_up, ada_b)
    xs = jnp.concatenate([x[0], ctx[0]], axis=0)
    u, u8, u_rs = _modulate(xs, mod, 0, 0, 1, n_lat, T)

    for l in range(L):
        last = l == L - 1
        p = _matmul(u, w_in_b, l, tm_mm, tn_mm(n_main))
        pg = _matmul_f8(u8, u_rs, w_gate8, w_gate_rs, l, tm_mm, _pick(3 * D, (1024, 512, 384, 256, 128)))
        hd = _local_mix(p, conv_dw, conv_dw_b, conv_ln_g, conv_ln_b, l, n_lat=n_lat, n_ctx=n_ctx, T=T, CW=CW)
        qk = _rope(p, cos, sin, perm, n_lat=n_lat, T=T, CW=CW, q_blk=3)
        na = _na(p, qk, bias, l, n_lat=n_lat, n_ctx=n_ctx, CW=CW, HG=HG)
        merged = _merge(hd, na, pg, w_conv_out_b, pool_w_b, w_na_out_b, pool_scale, l, CW=CW, tm=tm_mid)
        y = _matmul(merged, w_out_b, l, tm_mm, tn_mm(D))
        xs, uext = _residual_ln(xs, y, mod, l, 2, ln1_g, ln1_b, n_lat=n_lat, tm=T, alpha=alpha, rows=M,
                                nxt=(l, 3, 4), router=(wr_hi, wr_lo, br))
        f = _moe(uext, wgu, wd, ex, l, tA=tA)
        if last:
            (xs,) = _residual_ln(xs, f, mod, l, 5, ln2_g, ln2_b, n_lat=n_lat, tm=T, alpha=alpha, rows=n_lat)
        else:
            xs, u, u8, u_rs = _residual_ln(xs, f, mod, l, 5, ln2_g, ln2_b, n_lat=n_lat, tm=T, alpha=alpha, rows=M,
                                           nxt=(l + 1, 0, 1))
    return xs[None]
```

```python
import functools

import jax
import jax.numpy as jnp
from jax import lax
from jax.experimental import pallas as pl
from jax.experimental.pallas import tpu as pltpu

F32 = jnp.float32
BF16 = jnp.bfloat16
F8 = jnp.float8_e4m3fn
HIGHEST = lax.Precision.HIGHEST
F8_TARGET = 240.0
F8_TINY = 1e-30

GRID_W = 64
CONV_K = 31
POOL_WINDOWS = (2, 4, 8, 16)
HEAD_DIM = 128
NA_KH = 8
NA_KW = 16
N_GROUPS = 4
EXP_PER_GROUP = 8
N_EXPERTS = N_GROUPS * EXP_PER_GROUP
N_MOD = 6
ROPE_THETA = 10000.0
LN_EPS = 1e-5
NEG = -1e30

VMEM_LIMIT_BYTES = 56 * 1024 * 1024
LANES = 128
HALO = 16
ROUTER_LANES = 128
EXPERTS_PER_CHUNK = 2
NA_QROWS = NA_KH // 2
MOE_SUB_ROWS = 256


def _params(sem):
    return pltpu.CompilerParams(dimension_semantics=sem, vmem_limit_bytes=VMEM_LIMIT_BYTES)


def _pick(n, cands):
    for c in cands:
        if n % c == 0:
            return c
    raise ValueError(f"no tile in {cands} divides {n}")


def _sigmoid(x):
    return 1.0 / (1.0 + jnp.exp(-x))


def _adaln_kernel(c_ref, wd_ref, wu_ref, b_ref, o_ref):
    c = c_ref[...]
    s = c * _sigmoid(c)
    t = jnp.dot(s, wd_ref[...], precision=HIGHEST, preferred_element_type=F32)
    o_ref[...] = jnp.dot(t, wu_ref[...], precision=HIGHEST, preferred_element_type=F32) + b_ref[...]


def _adaln(cc, w_down, w_up, b):
    L, D, R = w_down.shape
    return pl.pallas_call(
        _adaln_kernel,
        out_shape=jax.ShapeDtypeStruct((L, 8, N_MOD * D), F32),
        grid=(L, N_MOD),
        in_specs=[
            pl.BlockSpec((8, D), lambda l, j: (0, 0)),
            pl.BlockSpec((None, D, R), lambda l, j: (l, 0, 0)),
            pl.BlockSpec((None, R, D), lambda l, j: (l, 0, j)),
            pl.BlockSpec((None, 1, D), lambda l, j: (l, 0, j)),
        ],
        out_specs=pl.BlockSpec((None, 8, D), lambda l, j: (l, 0, j)),
        compiler_params=_params(("parallel", "parallel")),
        name="adaln",
    )(cc, w_down, w_up, b.reshape(L, 1, N_MOD * D))


def _pick_row(ref, is_ctx):
    return jnp.where(is_ctx, ref[1:2, :], ref[0:1, :])


def _quant_rows(u):
    amax = jnp.max(jnp.abs(u), axis=-1, keepdims=True)
    scale = F8_TARGET / jnp.maximum(amax, F8_TINY)
    return (u * scale).astype(F8), 1.0 / scale


def _store_u(u, u8_ref, rs_ref):
    u8, inv = _quant_rows(u)
    u8_ref[...] = u8
    rs_ref[...] = jnp.broadcast_to(inv, rs_ref.shape)


def _u_outputs(rows, D, tm):
    shapes = [jax.ShapeDtypeStruct((rows, D), F8), jax.ShapeDtypeStruct((rows, LANES), F32)]
    specs = [pl.BlockSpec((tm, D), lambda i: (i, 0)), pl.BlockSpec((tm, LANES), lambda i: (i, 0))]
    return shapes, specs


def _modulate_kernel(x_ref, sh_ref, sc_ref, u8_ref, rs_ref, *, n_lat_tiles):
    is_ctx = pl.program_id(0) >= n_lat_tiles
    u = x_ref[...] * (1.0 + _pick_row(sc_ref, is_ctx)) + _pick_row(sh_ref, is_ctx)
    _store_u(u, u8_ref, rs_ref)


def _modulate(x, mod, l, k_sh, k_sc, n_lat, tm):
    M, D = x.shape
    shapes, specs = _u_outputs(M, D, tm)
    return pl.pallas_call(
        functools.partial(_modulate_kernel, n_lat_tiles=n_lat // tm),
        out_shape=shapes,
        grid=(M // tm,),
        in_specs=[
            pl.BlockSpec((tm, D), lambda i: (i, 0)),
            pl.BlockSpec((None, 8, D), lambda i: (l, 0, k_sh)),
            pl.BlockSpec((None, 8, D), lambda i: (l, 0, k_sc)),
        ],
        out_specs=specs,
        compiler_params=_params(("parallel",)),
        name="modulate",
    )(x, mod, mod)


def _route(logits):
    lane = lax.broadcasted_iota(jnp.int32, logits.shape, 1).astype(F32)
    big = float(ROUTER_LANES)
    gl = jnp.where(lane < N_GROUPS, logits, NEG)
    gmax = jnp.max(gl, axis=-1, keepdims=True)
    gidx = jnp.min(jnp.where(gl == gmax, lane, big), axis=-1, keepdims=True)
    p_group = 1.0 / jnp.sum(jnp.exp(gl - gmax), axis=-1, keepdims=True)
    e = lane - N_GROUPS
    in_group = (e >= gidx * EXP_PER_GROUP) & (e < (gidx + 1) * EXP_PER_GROUP)
    el = jnp.where(in_group, logits, NEG)
    t1 = jnp.max(el, axis=-1, keepdims=True)
    i1 = jnp.min(jnp.where(el == t1, lane, big), axis=-1, keepdims=True)
    el2 = jnp.where(lane == i1, NEG, el)
    t2 = jnp.max(el2, axis=-1, keepdims=True)
    i2 = jnp.min(jnp.where(el2 == t2, lane, big), axis=-1, keepdims=True)
    e2 = jnp.exp(t2 - t1)
    w1 = p_group / (1.0 + e2)
    w2 = p_group * e2 / (1.0 + e2)
    cw = jnp.where(lane == i1, w1, 0.0) + jnp.where(lane == i2, w2, 0.0)
    return jnp.where(lane == 0.0, gidx, cw)


def _ln_kernel(*refs, n_lat_tiles, alpha, with_u, with_router):
    x_ref, y_ref, gate_ref, lng_ref, lnb_ref = refs[:5]
    pos = 5
    if with_u:
        sh_ref, sc_ref = refs[pos:pos + 2]
        pos += 2
    if with_router:
        wrh_ref, wrl_ref, br_ref = refs[pos:pos + 3]
        pos += 3
    xo_ref = refs[pos]
    pos += 1
    is_ctx = pl.program_id(0) >= n_lat_tiles
    z = alpha * x_ref[...] + _pick_row(gate_ref, is_ctx) * y_ref[...].astype(F32)
    mu = jnp.mean(z, axis=-1, keepdims=True)
    zc = z - mu
    var = jnp.mean(zc * zc, axis=-1, keepdims=True)
    xn = zc * lax.rsqrt(var + LN_EPS) * lng_ref[...] + lnb_ref[...]
    xo_ref[...] = xn
    if with_u:
        u = xn * (1.0 + _pick_row(sc_ref, is_ctx)) + _pick_row(sh_ref, is_ctx)
        if with_router:
            u_ref = refs[pos]
            D = u.shape[-1]
            u_hi = u.astype(BF16)
            u_lo = (u - u_hi.astype(F32)).astype(BF16)
            wr_hi = wrh_ref[...]
            logits = (jnp.dot(u_hi, wr_hi, preferred_element_type=F32)
                      + jnp.dot(u_lo, wr_hi, preferred_element_type=F32)
                      + jnp.dot(u_hi, wrl_ref[...], preferred_element_type=F32) + br_ref[...])
            u_ref[:, :D] = u
            u_ref[:, D:] = _route(logits)
        else:
            _store_u(u, *refs[pos:pos + 2])


def _residual_ln(x, y, mod, l, k_gate, ln_g, ln_b, *, n_lat, tm, alpha, rows, nxt=None, router=None):
    M, D = x.shape
    L = ln_g.shape[0]
    in_specs = [
        pl.BlockSpec((tm, D), lambda i: (i, 0)),
        pl.BlockSpec((tm, D), lambda i: (i, 0)),
        pl.BlockSpec((None, 8, D), lambda i: (l, 0, k_gate)),
        pl.BlockSpec((None, 1, D), lambda i: (l, 0, 0)),
        pl.BlockSpec((None, 1, D), lambda i: (l, 0, 0)),
    ]
    args = [x, y, mod, ln_g.reshape(L, 1, D), ln_b.reshape(L, 1, D)]
    out_shape = [jax.ShapeDtypeStruct((rows, D), F32)]
    out_specs = [pl.BlockSpec((tm, D), lambda i: (i, 0))]
    if nxt is not None:
        ln, k_sh, k_sc = nxt
        in_specs += [pl.BlockSpec((None, 8, D), lambda i: (ln, 0, k_sh)),
                     pl.BlockSpec((None, 8, D), lambda i: (ln, 0, k_sc))]
        args += [mod, mod]
        if router is None:
            shapes, specs = _u_outputs(rows, D, tm)
            out_shape += shapes
            out_specs += specs
        else:
            out_shape.append(jax.ShapeDtypeStruct((rows, D + ROUTER_LANES), F32))
            out_specs.append(pl.BlockSpec((tm, D + ROUTER_LANES), lambda i: (i, 0)))
    if router is not None:
        wr_hi, wr_lo, br = router
        in_specs += [pl.BlockSpec((None, D, ROUTER_LANES), lambda i: (l, 0, 0)),
                     pl.BlockSpec((None, D, ROUTER_LANES), lambda i: (l, 0, 0)),
                     pl.BlockSpec((None, 1, ROUTER_LANES), lambda i: (l, 0, 0))]
        args += [wr_hi, wr_lo, br]
    return pl.pallas_call(
        functools.partial(_ln_kernel, n_lat_tiles=n_lat // tm, alpha=alpha,
                          with_u=nxt is not None, with_router=router is not None),
        out_shape=out_shape,
        grid=(rows // tm,),
        in_specs=in_specs,
        out_specs=out_specs,
        compiler_params=_params(("parallel",)),
        name="residual_ln",
    )(*args)


def _mm_kernel(a_ref, b_ref, o_ref):
    o_ref[...] = jnp.dot(a_ref[...], b_ref[...], preferred_element_type=F32).astype(o_ref.dtype)


def _matmul(a, b, l, tm, tn):
    M, K = a.shape
    N = b.shape[-1]
    return pl.pallas_call(
        _mm_kernel,
        out_shape=jax.ShapeDtypeStruct((M, N), BF16),
        grid=(M // tm, N // tn),
        in_specs=[pl.BlockSpec((tm, K), lambda i, j: (i, 0)),
                  pl.BlockSpec((None, K, tn), lambda i, j: (l, 0, j))],
        out_specs=pl.BlockSpec((tm, tn), lambda i, j: (i, j)),
        compiler_params=_params(("parallel", "arbitrary")),
        name="matmul",
    )(a, b)


def _mm_f8_kernel(a_ref, rs_ref, b_ref, cs_ref, o_ref):
    acc = jnp.dot(a_ref[...], b_ref[...], preferred_element_type=F32)
    o_ref[...] = (acc * rs_ref[:, 0:1] * cs_ref[...]).astype(o_ref.dtype)


def _matmul_f8(a8, rs, b8, cs, l, tm, tn):
    M, K = a8.shape
    N = b8.shape[-1]
    return pl.pallas_call(
        _mm_f8_kernel,
        out_shape=jax.ShapeDtypeStruct((M, N), BF16),
        grid=(M // tm, N // tn),
        in_specs=[pl.BlockSpec((tm, K), lambda i, j: (i, 0)),
                  pl.BlockSpec((tm, LANES), lambda i, j: (i, 0)),
                  pl.BlockSpec((None, K, tn), lambda i, j: (l, 0, j)),
                  pl.BlockSpec((None, 1, tn), lambda i, j: (l, 0, j))],
        out_specs=pl.BlockSpec((tm, tn), lambda i, j: (i, j)),
        compiler_params=_params(("parallel", "arbitrary")),
        name="matmul_f8",
    )(a8, rs, b8, cs)


def _local_kernel(prev_ref, cur_ref, next_ref, dw_ref, dwb_ref, lng_ref, lnb_ref, o_ref,
                  glu_s, z_s, y_s, *, T, CW, GW, n_lat, n_ctx):
    s = pl.program_id(0) * T
    in_ctx = s >= n_lat
    seq_start = jnp.where(in_ctx, n_lat, 0)
    seq_len = jnp.where(in_ctx, n_ctx, n_lat)
    keep_prev = jnp.where(s == seq_start, 0.0, 1.0)
    keep_next = jnp.where(s + T == seq_start + seq_len, 0.0, 1.0)

    def stage(ref, row0, rows, keep):
        blk = ref[...]
        a = blk[:, :CW].astype(F32)
        g = blk[:, CW:2 * CW].astype(F32)
        z = blk[:, 2 * CW:3 * CW].astype(F32)
        glu = a * _sigmoid(g)
        if keep is not None:
            glu = glu * keep
            z = z * keep
        glu_s[row0:row0 + rows, :] = glu
        z_s[row0:row0 + rows, :] = z

    stage(prev_ref, 0, HALO, keep_prev)
    stage(cur_ref, HALO, T, None)
    stage(next_ref, HALO + T, HALO, keep_next)

    RC = min(T, 128)
    base = HALO - CONV_K // 2
    for rc in range(T // RC):
        for lc in range(CW // LANES):
            cols = slice(lc * LANES, (lc + 1) * LANES)
            acc = jnp.zeros((RC, LANES), F32)
            for j in range(CONV_K):
                r0 = rc * RC + base + j
                acc = acc + glu_s[r0:r0 + RC, cols] * dw_ref[j:j + 1, cols]
            y_s[rc * RC:(rc + 1) * RC, cols] = acc + dwb_ref[:, cols]

    y = y_s[...]
    mu = jnp.mean(y, axis=-1, keepdims=True)
    yc = y - mu
    var = jnp.mean(yc * yc, axis=-1, keepdims=True)
    hn = yc * lax.rsqrt(var + LN_EPS) * lng_ref[...] + lnb_ref[...]
    o_ref[:, :CW] = (hn * _sigmoid(hn)).astype(BF16)

    pos = lax.broadcasted_iota(jnp.int32, (T, 1), 0) + (s - seq_start)
    for gi, w in enumerate(POOL_WINDOWS):
        cols = slice(gi * GW, (gi + 1) * GW)
        tot = jnp.zeros((T, GW), F32)
        for o in range(-(w // 2), w - w // 2):
            tot = tot + z_s[HALO + o:HALO + o + T, cols]
        cnt = jnp.minimum(pos + (w - w // 2), seq_len) - jnp.maximum(pos - w // 2, 0)
        d = tot / cnt.astype(F32) - z_s[HALO:HALO + T, cols]
        o_ref[:, CW + gi * GW:CW + (gi + 1) * GW] = d.astype(BF16)


def _local_mix(p, conv_dw, conv_dw_b, conv_ln_g, conv_ln_b, l, *, n_lat, n_ctx, T, CW):
    M = p.shape[0]
    L = conv_dw.shape[0]
    GW = CW // len(POOL_WINDOWS)
    hb = T // HALO
    n_hblk = M // HALO
    return pl.pallas_call(
        functools.partial(_local_kernel, T=T, CW=CW, GW=GW, n_lat=n_lat, n_ctx=n_ctx),
        out_shape=jax.ShapeDtypeStruct((M, 2 * CW), BF16),
        grid=(M // T,),
        in_specs=[
            pl.BlockSpec((HALO, 3 * CW), lambda i: (jnp.maximum(i * hb - 1, 0), 0)),
            pl.BlockSpec((T, 3 * CW), lambda i: (i, 0)),
            pl.BlockSpec((HALO, 3 * CW), lambda i: (jnp.minimum((i + 1) * hb, n_hblk - 1), 0)),
            pl.BlockSpec((None, CONV_K, CW), lambda i: (l, 0, 0)),
            pl.BlockSpec((None, 1, CW), lambda i: (l, 0, 0)),
            pl.BlockSpec((None, 1, CW), lambda i: (l, 0, 0)),
            pl.BlockSpec((None, 1, CW), lambda i: (l, 0, 0)),
        ],
        out_specs=pl.BlockSpec((T, 2 * CW), lambda i: (i, 0)),
        scratch_shapes=[pltpu.VMEM((T + 2 * HALO, CW), F32),
                        pltpu.VMEM((T + 2 * HALO, CW), F32),
                        pltpu.VMEM((T, CW), F32)],
        compiler_params=_params(("parallel",)),
        name="local_mix",
    )(p, p, p, conv_dw, conv_dw_b.reshape(L, 1, CW), conv_ln_g.reshape(L, 1, CW),
      conv_ln_b.reshape(L, 1, CW))


def _rope_kernel(x_ref, cos_ref, sin_ref, perm_ref, o_ref, *, CW):
    cos = cos_ref[...]
    sin = sin_ref[...]
    perm = perm_ref[...]
    for h in range(CW // HEAD_DIM):
        cols = slice(h * HEAD_DIM, (h + 1) * HEAD_DIM)
        xb = x_ref[:, cols]
        partner = jnp.dot(xb, perm, preferred_element_type=F32)
        o_ref[:, cols] = (xb.astype(F32) * cos + partner * sin).astype(BF16)


def _rope_tables(n_lat):
    t = jnp.arange(n_lat)
    r = (t // GRID_W).astype(F32)
    col = (t % GRID_W).astype(F32)
    axis_dim = HEAD_DIM // 2
    inv = ROPE_THETA ** (-jnp.arange(0, axis_dim, 2, dtype=F32) / axis_dim)
    ang_r = r[:, None] * inv[None, :]
    ang_c = col[:, None] * inv[None, :]
    cos = jnp.concatenate([jnp.cos(ang_r), jnp.cos(ang_r), jnp.cos(ang_c), jnp.cos(ang_c)], axis=-1)
    sin = jnp.concatenate([-jnp.sin(ang_r), jnp.sin(ang_r), -jnp.sin(ang_c), jnp.sin(ang_c)], axis=-1)
    q = axis_dim // 2
    i = jnp.arange(HEAD_DIM)
    src = jnp.where((i % axis_dim) < q, i + q, i - q)
    perm = (jnp.arange(HEAD_DIM)[:, None] == src[None, :]).astype(BF16)
    return cos, sin, perm


def _rope(p, cos, sin, perm, *, n_lat, T, CW, q_blk):
    return pl.pallas_call(
        functools.partial(_rope_kernel, CW=CW),
        out_shape=jax.ShapeDtypeStruct((n_lat, 4 * CW), BF16),
        grid=(n_lat // T, 4),
        in_specs=[
            pl.BlockSpec((T, CW), lambda i, j: (i, q_blk + j)),
            pl.BlockSpec((T, HEAD_DIM), lambda i, j: (i, 0)),
            pl.BlockSpec((T, HEAD_DIM), lambda i, j: (i, 0)),
            pl.BlockSpec((HEAD_DIM, HEAD_DIM), lambda i, j: (0, 0)),
        ],
        out_specs=pl.BlockSpec((T, CW), lambda i, j: (i, j)),
        compiler_params=_params(("parallel", "arbitrary")),
        name="rope",
    )(p, cos, sin, perm)


def _na_bias_tables(rpb):
    W = GRID_W
    w = jnp.arange(W)
    c0 = jnp.clip(w - NA_KW // 2, 0, W - NA_KW)
    kc = jnp.arange(W)
    dc = kc[None, :] - w[:, None] + (NA_KW - 1)
    valid = (kc[None, :] >= c0[:, None]) & (kc[None, :] < c0[:, None] + NA_KW)
    onehot = (jnp.arange(2 * NA_KW - 1)[:, None, None] == dc[None, :, :]) & valid[None]
    bt = jnp.einsum('lhrd,dwk->lhrwk', rpb, onehot.astype(F32), precision=HIGHEST)
    bt = jnp.where(valid[None, None, None], bt, NEG)
    QR = NA_QROWS
    L, H = rpb.shape[:2]
    return pl.pallas_call(
        _na_bias_kernel,
        out_shape=jax.ShapeDtypeStruct((L, 3, H, QR * W, 3 * QR * W), F32),
        grid=(L, H),
        in_specs=[pl.BlockSpec((None, None, 2 * NA_KH - 1, W, W), lambda l, h: (l, h, 0, 0, 0))],
        out_specs=pl.BlockSpec((None, 3, None, QR * W, 3 * QR * W), lambda l, h: (l, 0, h, 0, 0)),
        compiler_params=_params(("parallel", "parallel")),
        name="na_bias_tables",
    )(bt)


def _na_bias_kernel(bt_ref, o_ref):
    QR, W = NA_QROWS, GRID_W
    neg = jnp.full((W, W), NEG, F32)
    for cls in range(3):
        for a in range(QR):
            cells = []
            for b in range(3 * QR):
                valid = (0 <= b - a < NA_KH, QR <= b < QR + NA_KH, b < NA_KH)[cls]
                cells.append(bt_ref[b - a + NA_KH - 1 - QR] if valid else neg)
            o_ref[cls, a * W:(a + 1) * W, :] = jnp.concatenate(cells, axis=1)


def _dot_nt(a, b):
    return lax.dot_general(a, b, (((1,), (1,)), ((), ())), preferred_element_type=F32)


def _na_kernel(qr_ref, qp_ref, kp_ref, kcur_ref, kn_ref, vp_ref, vcur_ref, vn_ref, kc_ref, vc_ref,
               bias_ref, o_ref, *, HG, nblk, scale):
    rb = pl.program_id(1)
    B = NA_QROWS * GRID_W

    @pl.when(rb < nblk)
    def _():
        for h in range(HG):
            cols = slice(h * HEAD_DIM, (h + 1) * HEAD_DIM)
            q = qr_ref[:, cols]
            s = jnp.concatenate([_dot_nt(q, kp_ref[:, cols]), _dot_nt(q, kcur_ref[:, cols]),
                                 _dot_nt(q, kn_ref[:, cols])], axis=1) * scale + bias_ref[h]
            sc = _dot_nt(qp_ref[:, cols], kc_ref[:, cols]) * scale
            m = jnp.maximum(jnp.max(s, axis=-1, keepdims=True), jnp.max(sc, axis=-1, keepdims=True))
            e = jnp.exp(s - m)
            ec = jnp.exp(sc - m)
            den = jnp.sum(e, axis=-1, keepdims=True) + jnp.sum(ec, axis=-1, keepdims=True)
            eb = e.astype(BF16)
            o = (jnp.dot(eb[:, 0:B], vp_ref[:, cols], preferred_element_type=F32)
                 + jnp.dot(eb[:, B:2 * B], vcur_ref[:, cols], preferred_element_type=F32)
                 + jnp.dot(eb[:, 2 * B:3 * B], vn_ref[:, cols], preferred_element_type=F32)
                 + jnp.dot(ec.astype(BF16), vc_ref[:, cols], preferred_element_type=F32))
            o_ref[:, cols] = (o / den).astype(BF16)

    @pl.when(rb == nblk)
    def _():
        for h in range(HG):
            cols = slice(h * HEAD_DIM, (h + 1) * HEAD_DIM)
            s = _dot_nt(qp_ref[:, cols], kc_ref[:, cols]) * scale
            m = jnp.max(s, axis=-1, keepdims=True)
            e = jnp.exp(s - m)
            o = jnp.dot(e.astype(BF16), vc_ref[:, cols], preferred_element_type=F32)
            o_ref[:, cols] = (o / jnp.sum(e, axis=-1, keepdims=True)).astype(BF16)


def _na(p, qk, bias, l, *, n_lat, n_ctx, CW, HG):
    M = p.shape[0]
    NA_W = 2 * CW
    heads = NA_W // HEAD_DIM
    blk = NA_QROWS * GRID_W
    nblk = n_lat // blk
    assert n_ctx == blk and nblk >= 2
    bw = HG * HEAD_DIM
    q_c, k_c, v_c = 3 * CW // bw, 5 * CW // bw, 7 * CW // bw
    kr_c = NA_W // bw
    ctx_blk = n_lat // n_ctx
    cur = lambda r: jnp.minimum(r, nblk - 1)
    prev = lambda r: jnp.maximum(cur(r) - 1, 0)
    nxt = lambda r: jnp.minimum(r + 1, nblk - 1)
    cls = lambda r: jnp.where(r == 0, 1, jnp.where(r >= nblk - 1, 2, 0))
    return pl.pallas_call(
        functools.partial(_na_kernel, HG=HG, nblk=nblk, scale=HEAD_DIM ** -0.5),
        out_shape=jax.ShapeDtypeStruct((M, NA_W), BF16),
        grid=(heads // HG, nblk + 1),
        in_specs=[
            pl.BlockSpec((blk, bw), lambda g, r: (cur(r), g)),
            pl.BlockSpec((blk, bw), lambda g, r: (r, q_c + g)),
            pl.BlockSpec((blk, bw), lambda g, r: (prev(r), kr_c + g)),
            pl.BlockSpec((blk, bw), lambda g, r: (cur(r), kr_c + g)),
            pl.BlockSpec((blk, bw), lambda g, r: (nxt(r), kr_c + g)),
            pl.BlockSpec((blk, bw), lambda g, r: (prev(r), v_c + g)),
            pl.BlockSpec((blk, bw), lambda g, r: (cur(r), v_c + g)),
            pl.BlockSpec((blk, bw), lambda g, r: (nxt(r), v_c + g)),
            pl.BlockSpec((n_ctx, bw), lambda g, r: (ctx_blk, k_c + g)),
            pl.BlockSpec((n_ctx, bw), lambda g, r: (ctx_blk, v_c + g)),
            pl.BlockSpec((None, None, HG, blk, 3 * blk), lambda g, r: (l, cls(r), g, 0, 0)),
        ],
        out_specs=pl.BlockSpec((blk, bw), lambda g, r: (r, g)),
        compiler_params=_params(("parallel", "arbitrary")),
        name="neighbourhood_attention",
    )(qk, p, qk, qk, qk, p, p, p, p, p, bias)


def _merge_kernel(h_ref, d_ref, na_ref, g0_ref, g1_ref, g2_ref, wc_ref, wp_ref, wn_ref, ps_ref, o_ref):
    yc = jnp.dot(h_ref[...], wc_ref[...], preferred_element_type=F32)
    yp = jnp.dot(d_ref[...], wp_ref[...], preferred_element_type=F32) * ps_ref[...]
    yn = jnp.dot(na_ref[...], wn_ref[...], preferred_element_type=F32)
    merged = (_sigmoid(g0_ref[...].astype(F32)) * yc + _sigmoid(g1_ref[...].astype(F32)) * yp
              + _sigmoid(g2_ref[...].astype(F32)) * yn)
    o_ref[...] = merged.astype(BF16)


def _merge(hd, na, p, w_conv_out, pool_w, w_na_out, pool_scale, l, *, CW, tm):
    M = hd.shape[0]
    D = 4 * CW
    L = pool_scale.shape[0]
    G = len(POOL_WINDOWS)
    GW = CW // G
    gate_blk = 9
    return pl.pallas_call(
        _merge_kernel,
        out_shape=jax.ShapeDtypeStruct((M, D), BF16),
        grid=(M // tm, G),
        in_specs=[
            pl.BlockSpec((tm, CW), lambda i, j: (i, 0)),
            pl.BlockSpec((tm, GW), lambda i, j: (i, G + j)),
            pl.BlockSpec((tm, 2 * CW), lambda i, j: (i, 0)),
            pl.BlockSpec((tm, CW), lambda i, j: (i, gate_blk + j)),
            pl.BlockSpec((tm, CW), lambda i, j: (i, gate_blk + G + j)),
            pl.BlockSpec((tm, CW), lambda i, j: (i, gate_blk + 2 * G + j)),
            pl.BlockSpec((None, CW, CW), lambda i, j: (l, 0, j)),
            pl.BlockSpec((None, None, GW, CW), lambda i, j: (l, j, 0, 0)),
            pl.BlockSpec((None, 2 * CW, CW), lambda i, j: (l, 0, j)),
            pl.BlockSpec((None, 1, CW), lambda i, j: (l, 0, j)),
        ],
        out_specs=pl.BlockSpec((tm, CW), lambda i, j: (i, j)),
        compiler_params=_params(("parallel", "arbitrary")),
        name="merge_branches",
    )(hd, hd, na, p, p, p, w_conv_out, pool_w, w_na_out, pool_scale.reshape(L, 1, D))


def _moe_kernel(tg_ref, src_ref, dst_ref, u_hbm, wgu_ref, wgs_ref, wd_ref, ex_ref, f_hbm,
                gbuf, ubf, rsb, cwb, acc, obuf, sem_in, sem_out, *, tA, D, EH, n_tiles, NC):
    del tg_ref
    t = pl.program_id(0)
    c = pl.program_id(1)

    def start_gather(tile):
        def body(r, carry):
            tok = src_ref[tile * tA + r]
            pltpu.make_async_copy(u_hbm.at[pl.ds(tok, 1)], gbuf.at[pl.ds(r, 1)], sem_in).start()
            return carry
        lax.fori_loop(0, tA, body, 0, unroll=8)

    def start_scatter(tile):
        def body(r, carry):
            tok = dst_ref[tile * tA + r]
            pltpu.make_async_copy(obuf.at[pl.ds(r, 1)], f_hbm.at[pl.ds(tok, 1)], sem_out).start()
            return carry
        lax.fori_loop(0, tA, body, 0, unroll=8)

    def wait_rows(buf, sem):
        pltpu.make_async_copy(buf, buf, sem).wait()

    @pl.when((t == 0) & (c == 0))
    def _():
        start_gather(0)
        acc[...] = jnp.zeros_like(acc)

    @pl.when(c == 0)
    def _():
        wait_rows(gbuf, sem_in)
        u8, inv = _quant_rows(gbuf[:, :D])
        ubf[...] = u8
        rsb[...] = jnp.broadcast_to(inv, rsb.shape)
        cwb[...] = gbuf[:, D:].astype(BF16)

        @pl.when(t + 1 < n_tiles)
        def _():
            start_gather(t + 1)

    for sb in range(tA // MOE_SUB_ROWS):
        rows = slice(sb * MOE_SUB_ROWS, (sb + 1) * MOE_SUB_ROWS)
        gu = jnp.dot(ubf[rows, :], wgu_ref[...], preferred_element_type=F32) * rsb[rows, 0:1] * wgs_ref[...]
        g = gu[:, :EH]
        up = gu[:, EH:]
        cwx = jnp.dot(cwb[rows, :], ex_ref[...], preferred_element_type=F32)
        h = (g * _sigmoid(g)) * up * cwx
        contrib = jnp.dot(h.astype(BF16), wd_ref[...], preferred_element_type=F32)
        acc[rows, :] = jnp.where(c == 0, 0.0, acc[rows, :]) + contrib

    @pl.when(c == NC - 1)
    def _():
        @pl.when(t > 0)
        def _():
            wait_rows(obuf, sem_out)

        obuf[...] = acc[...]
        start_scatter(t)

        @pl.when(t == n_tiles - 1)
        def _():
            wait_rows(obuf, sem_out)


def _moe_plan(group, M, tA, n_tiles):
    onehot = (group[:, None] == jnp.arange(N_GROUPS)[None, :]).astype(jnp.int32)
    rank = jnp.sum((jnp.cumsum(onehot, axis=0) - onehot) * onehot, axis=1)
    tiles_g = (jnp.sum(onehot, axis=0) + tA - 1) // tA
    first_tile = jnp.cumsum(tiles_g) - tiles_g
    pos = first_tile[group] * tA + rank
    tok = jnp.arange(M, dtype=jnp.int32)
    slot_tok = jnp.full((n_tiles * tA,), -1, jnp.int32).at[pos].set(tok)
    src = jnp.maximum(slot_tok, 0)
    dst = jnp.where(slot_tok < 0, M + jnp.arange(n_tiles * tA, dtype=jnp.int32) % tA, slot_tok)
    tile_group = jnp.sum(jnp.arange(n_tiles)[:, None] >= first_tile[None, :], axis=1) - 1
    return tile_group.astype(jnp.int32), src, dst


def _moe(uext, wgu, wgs, wd, ex, l, *, tA):
    M = uext.shape[0]
    D = uext.shape[1] - ROUTER_LANES
    EH = wd.shape[2]
    NC = EXP_PER_GROUP // EXPERTS_PER_CHUNK
    n_tiles = (M + N_GROUPS * (tA - 1)) // tA
    tile_group, src, dst = _moe_plan(uext[:, D].astype(jnp.int32), M, tA, n_tiles)
    grid_spec = pltpu.PrefetchScalarGridSpec(
        num_scalar_prefetch=3,
        grid=(n_tiles, NC),
        in_specs=[
            pl.BlockSpec(memory_space=pl.ANY),
            pl.BlockSpec((None, None, D, 2 * EH), lambda t, c, tg, s, d: (l, tg[t] * NC + c, 0, 0)),
            pl.BlockSpec((None, None, 1, 2 * EH), lambda t, c, tg, s, d: (l, tg[t] * NC + c, 0, 0)),
            pl.BlockSpec((None, None, EH, D), lambda t, c, tg, s, d: (l, tg[t] * NC + c, 0, 0)),
            pl.BlockSpec((None, ROUTER_LANES, EH), lambda t, c, tg, s, d: (tg[t] * NC + c, 0, 0)),
        ],
        out_specs=pl.BlockSpec(memory_space=pl.ANY),
        scratch_shapes=[
            pltpu.VMEM((tA, D + ROUTER_LANES), F32),
            pltpu.VMEM((tA, D), F8),
            pltpu.VMEM((tA, LANES), F32),
            pltpu.VMEM((tA, ROUTER_LANES), BF16),
            pltpu.VMEM((tA, D), F32),
            pltpu.VMEM((tA, D), F32),
            pltpu.SemaphoreType.DMA(()),
            pltpu.SemaphoreType.DMA(()),
        ],
    )
    return pl.pallas_call(
        functools.partial(_moe_kernel, tA=tA, D=D, EH=EH, n_tiles=n_tiles, NC=NC),
        out_shape=jax.ShapeDtypeStruct((M + tA, D), F32),
        grid_spec=grid_spec,
        compiler_params=_params(("arbitrary", "arbitrary")),
        name="moe_experts",
    )(tile_group, src, dst, uext, wgu, wgs, wd, ex)


def kernel(x, c, ctx, c_ctx, ada_w_down, ada_w_up, ada_b, w_in, conv_dw, conv_dw_b, conv_ln_g,
           conv_ln_b, w_conv_out, pool_w, pool_scale, na_rpb, w_na_out, w_out, ln1_g, ln1_b,
           router_g_w, router_g_b, router_e_w, router_e_b, moe_w_gate, moe_w_up, moe_w_down,
           ln2_g, ln2_b):
    B, n_lat, D = x.shape
    n_ctx = ctx.shape[1]
    L = w_in.shape[0]
    assert B == 1 and D % 16 == 0
    CW = D // 4
    M = n_lat + n_ctx
    rows = n_lat // GRID_W
    H = moe_w_gate.shape[-1]
    alpha = (2 * L) ** 0.25
    assert n_lat % (NA_QROWS * GRID_W) == 0 and rows >= NA_KH and n_lat % n_ctx == 0
    assert (CW // len(POOL_WINDOWS)) % 8 == 0 and CW % HEAD_DIM == 0

    T = _pick(n_ctx, (256, 128, 64))
    tm_mm = _pick(M, (1280, 640, 320, 256, 128))
    tn_mm = lambda n: _pick(n, (512, 384, 256, 128))
    tm_mid = _pick(M, (640, 320, 256, 128))
    tA = min(512, T * 2)
    HG = _pick(CW // HEAD_DIM, (4, 2, 1))

    cc = jnp.zeros((8, D), F32).at[0].set(c[0]).at[1].set(c_ctx)
    col_scale = F8_TARGET / jnp.maximum(jnp.max(jnp.abs(w_in), axis=1, keepdims=True), F8_TINY)
    w_in8 = (w_in * col_scale).astype(F8)
    w_in_rs = 1.0 / col_scale
    w_conv_out_b = w_conv_out.astype(BF16)
    pool_w_b = pool_w.astype(BF16)
    w_na_out_b = w_na_out.astype(BF16)
    w_out_b = w_out.astype(BF16)
    EC = EXPERTS_PER_CHUNK
    NCH = N_EXPERTS // EC

    def chunked(w):
        scale = F8_TARGET / jnp.maximum(jnp.max(jnp.abs(w), axis=2, keepdims=True), F8_TINY)
        w8 = (w * scale).astype(F8)
        w8 = jnp.transpose(w8.reshape(L, NCH, EC, D, H), (0, 1, 3, 2, 4)).reshape(L, NCH, D, EC * H)
        return w8, (1.0 / scale).reshape(L, NCH, 1, EC * H)

    wg8, wg_rs = chunked(moe_w_gate)
    wu8, wu_rs = chunked(moe_w_up)
    wgu = jnp.concatenate([wg8, wu8], axis=-1)
    wgs = jnp.concatenate([wg_rs, wu_rs], axis=-1)
    wd = moe_w_down.astype(BF16).reshape(L, NCH, EC * H, D)
    lane = jnp.arange(ROUTER_LANES)[None, :, None]
    col_expert = (jnp.arange(NCH)[:, None, None] * EC + jnp.arange(EC * H)[None, None, :] // H)
    ex = (lane == N_GROUPS + col_expert).astype(BF16)
    pad = ROUTER_LANES - N_GROUPS - N_EXPERTS
    wr = jnp.concatenate([router_g_w, router_e_w, jnp.zeros((L, D, pad), F32)], axis=-1)
    wr_hi = wr.astype(BF16)
    wr_lo = (wr - wr_hi.astype(F32)).astype(BF16)
    br = jnp.concatenate([router_g_b, router_e_b, jnp.zeros((L, pad), F32)], axis=-1).reshape(L, 1, ROUTER_LANES)
    bias = _na_bias_tables(na_rpb)
    cos, sin, perm = _rope_tables(n_lat)

    mod = _adaln(cc, ada_w_down, ada_w_up, ada_b)
    xs = jnp.concatenate([x[0], ctx[0]], axis=0)
    u8, u_rs = _modulate(xs, mod, 0, 0, 1, n_lat, T)

    for l in range(L):
        last = l == L - 1
        p = _matmul_f8(u8, u_rs, w_in8, w_in_rs, l, tm_mm, _pick(w_in.shape[-1], (1024, 512, 384, 256, 128)))
        hd = _local_mix(p, conv_dw, conv_dw_b, conv_ln_g, conv_ln_b, l, n_lat=n_lat, n_ctx=n_ctx, T=T, CW=CW)
        qk = _rope(p, cos, sin, perm, n_lat=n_lat, T=T, CW=CW, q_blk=3)
        na = _na(p, qk, bias, l, n_lat=n_lat, n_ctx=n_ctx, CW=CW, HG=HG)
        merged = _merge(hd, na, p, w_conv_out_b, pool_w_b, w_na_out_b, pool_scale, l, CW=CW, tm=tm_mid)
        y = _matmul(merged, w_out_b, l, tm_mm, tn_mm(D))
        xs, uext = _residual_ln(xs, y, mod, l, 2, ln1_g, ln1_b, n_lat=n_lat, tm=T, alpha=alpha, rows=M,
                                nxt=(l, 3, 4), router=(wr_hi, wr_lo, br))
        f = _moe(uext, wgu, wgs, wd, ex, l, tA=tA)
        if last:
            (xs,) = _residual_ln(xs, f, mod, l, 5, ln2_g, ln2_b, n_lat=n_lat, tm=T, alpha=alpha, rows=n_lat)
        else:
            xs, u8, u_rs = _residual_ln(xs, f, mod, l, 5, ln2_g, ln2_b, n_lat=n_lat, tm=T, alpha=alpha, rows=M,
                                           nxt=(l + 1, 0, 1))
    return xs[None]
```
